```python
import math
import jax, jax.numpy as jnp
from jax import lax
import numpy as np

D_MODEL = 1024
BATCH = 8
SEQ = 2048
DEPTH = 4

CHUNK = 64
N_MEM = 256
N_MIXERS = 3
MIXER_W = D_MODEL
GDN_DK = 128
GDN_DV = 128
GDN_HEADS = MIXER_W // GDN_DV
GDN_KW = GDN_HEADS * GDN_DK
GDN_VW = GDN_HEADS * GDN_DV
GDN_QKV = 2 * GDN_KW + GDN_VW
CONV_K = 4
ML_DQK = 64
ML_DV = 128
ML_HEADS = MIXER_W // ML_DV
ML_KW = ML_HEADS * ML_DQK
ML_VW = ML_HEADS * ML_DV
SB_DH = 64
SB_HEADS = MIXER_W // SB_DH
SB_W = SB_HEADS * SB_DH
SB_BLOCK = 128
MEM_HEADS = 4
MEM_DH = 128
MEM_W = MEM_HEADS * MEM_DH
MIX_W = MIXER_W + MEM_W
GDN_IN = GDN_QKV + GDN_VW + 2 * GDN_HEADS + MEM_W
ML_IN = 2 * ML_KW + 2 * ML_VW + 2 * ML_HEADS + MEM_W
SB_IN = 3 * SB_W + MEM_W
D_FF = -(-8 * D_MODEL // (3 * 256)) * 256
N_GDN = (DEPTH + 2) // N_MIXERS
N_ML = (DEPTH + 1) // N_MIXERS
N_SB = DEPTH // N_MIXERS
EPS = 1e-6

kernel_name = "hybrid_gdn_mlstm_stickbreak_mem_trunk"


def rms_norm(x, g):
    xf = x.astype(jnp.float32)
    y = xf * lax.rsqrt(jnp.mean(xf * xf, axis=-1, keepdims=True) + EPS)
    return (y * g.astype(jnp.float32)).astype(x.dtype)


def l2norm(x):
    return x * lax.rsqrt(jnp.sum(x * x, axis=-1, keepdims=True) + EPS)


def split_heads(t, n):
    b, s, _ = t.shape
    return t.reshape(b, s, n, -1).transpose(0, 2, 1, 3)


def merge_heads(t):
    b, h, s, d = t.shape
    return t.transpose(0, 2, 1, 3).reshape(b, s, h * d)


def to_chunks(t):
    b, h, s = t.shape[:3]
    return t.reshape(b, h, s // CHUNK, CHUNK, *t.shape[3:])


def causal_dwconv(x, w):
    k, c = w.shape
    return lax.conv_general_dilated(x, w[:, None, :].astype(x.dtype), window_strides=(1,),
                                    padding=[(k - 1, 0)], dimension_numbers=("NWC", "WIO", "NWC"),
                                    feature_group_count=c)


def gated_delta_rule(q, k, v, a, b, a_log, dt_bias):
    f32 = jnp.float32
    q, k, v, a, b = (t.astype(f32) for t in (q, k, v, a, b))
    q = l2norm(q) * GDN_DK ** -0.5
    k = l2norm(k)
    beta = jax.nn.sigmoid(b)
    g = -jnp.exp(a_log.astype(f32))[None, :, None] * jax.nn.softplus(a + dt_bias.astype(f32)[None, :, None])
    qc, kc, vc, bc, gc = (to_chunks(t) for t in (q, k, v, beta, g))
    G = jnp.cumsum(gc, axis=-1)
    incl = jnp.tril(jnp.ones((CHUNK, CHUNK), dtype=bool))
    strict = jnp.tril(jnp.ones((CHUNK, CHUNK), dtype=bool), -1)
    diff = G[..., :, None] - G[..., None, :]
    decay = jnp.where(incl, jnp.exp(jnp.where(incl, diff, 0.0)), 0.0)
    kk = jnp.einsum("bhnid,bhnjd->bhnij", kc, kc)
    lower = jnp.where(strict, bc[..., :, None] * kk * decay, 0.0)
    rhs = jnp.concatenate([vc * bc[..., None], kc * (bc * jnp.exp(G))[..., None]], axis=-1)
    sol = lax.linalg.triangular_solve(lower + jnp.eye(CHUNK, dtype=f32), rhs, left_side=True,
                                      lower=True, unit_diagonal=True)
    u, w = sol[..., :GDN_DV], sol[..., GDN_DV:]
    p = jnp.einsum("bhnid,bhnjd->bhnij", qc, kc) * decay
    qg = qc * jnp.exp(G)[..., None]
    kg = kc * jnp.exp(G[..., -1:] - G)[..., None]
    gl = jnp.exp(G[..., -1])

    def step(state, xs):
        u_n, w_n, qg_n, kg_n, p_n, gl_n = xs
        v_new = u_n - jnp.einsum("bhcd,bhde->bhce", w_n, state)
        o_n = jnp.einsum("bhcd,bhde->bhce", qg_n, state) + jnp.einsum("bhij,bhje->bhie", p_n, v_new)
        state = state * gl_n[..., None, None] + jnp.einsum("bhcd,bhce->bhde", kg_n, v_new)
        return state, o_n

    bn, hn = q.shape[:2]
    s0 = jnp.zeros((bn, hn, GDN_DK, GDN_DV), f32)
    xs = tuple(jnp.moveaxis(t, 2, 0) for t in (u, w, qg, kg, p, gl))
    _, o = lax.scan(step, s0, xs)
    return jnp.moveaxis(o, 0, 2).reshape(bn, hn, -1, GDN_DV)


def mlstm_cell(q, k, v, i_pre, f_pre):
    f32 = jnp.float32
    q, k, v, i_pre, f_pre = (t.astype(f32) for t in (q, k, v, i_pre, f_pre))
    k = k * ML_DQK ** -0.5
    lf = jax.nn.log_sigmoid(f_pre)
    qc, kc, vc, lic, lfc = (to_chunks(t) for t in (q, k, v, i_pre, lf))
    bcum = jnp.cumsum(lfc, axis=-1)
    incl = jnp.tril(jnp.ones((CHUNK, CHUNK), dtype=bool))
    dmat = jnp.where(incl, bcum[..., :, None] - bcum[..., None, :] + lic[..., None, :], -jnp.inf)
    m_intra = jnp.max(dmat, axis=-1)
    sqk = jnp.einsum("bhnid,bhnjd->bhnij", qc, kc) * jnp.exp(dmat - m_intra[..., None])
    num_intra = jnp.einsum("bhnij,bhnje->bhnie", sqk, vc)
    den_intra = jnp.sum(sqk, axis=-1)
    bl = bcum[..., -1]
    wk = bl[..., None] - bcum + lic
    m_chunk = jnp.max(wk, axis=-1)
    e = jnp.exp(wk - m_chunk[..., None])
    kv_c = jnp.einsum("bhnc,bhncd,bhnce->bhnde", e, kc, vc)
    ks_c = jnp.einsum("bhnc,bhncd->bhnd", e, kc)

    def step(carry, xs):
        c_s, n_s, m_s = carry
        kv_n, ks_n, bl_n, mc_n = xs
        m_new = jnp.maximum(bl_n + m_s, mc_n)
        a = jnp.exp(bl_n + m_s - m_new)
        c = jnp.exp(mc_n - m_new)
        c_new = a[..., None, None] * c_s + c[..., None, None] * kv_n
        n_new = a[..., None] * n_s + c[..., None] * ks_n
        return (c_new, n_new, m_new), (c_s, n_s, m_s)

    bn, hn = q.shape[:2]
    init = (jnp.zeros((bn, hn, ML_DQK, ML_DV), f32), jnp.zeros((bn, hn, ML_DQK), f32),
            jnp.zeros((bn, hn), f32))
    xs = tuple(jnp.moveaxis(t, 2, 0) for t in (kv_c, ks_c, bl, m_chunk))
    _, (c_prev, n_prev, m_prev) = lax.scan(step, init, xs)
    c_prev = jnp.moveaxis(c_prev, 0, 2)
    n_prev = jnp.moveaxis(n_prev, 0, 2)
    m_prev = jnp.moveaxis(m_prev, 0, 2)
    a_inter = bcum + m_prev[..., None]
    m_t = jnp.maximum(a_inter, m_intra)
    s_inter = jnp.exp(a_inter - m_t)
    s_intra = jnp.exp(m_intra - m_t)
    num = s_inter[..., None] * jnp.einsum("bhncd,bhnde->bhnce", qc, c_prev) + s_intra[..., None] * num_intra
    den = s_inter * jnp.einsum("bhncd,bhnd->bhnc", qc, n_prev) + s_intra * den_intra
    h = num / jnp.maximum(jnp.abs(den), jnp.exp(-m_t))[..., None]
    return h.reshape(bn, hn, -1, ML_DV)


def stick_breaking(q, k, v):
    seq = q.shape[2]
    scale = SB_DH ** -0.5
    outs = []
    for blk in range(seq // SB_BLOCK):
        q0 = blk * SB_BLOCK
        kend = q0 + SB_BLOCK
        z = jnp.einsum("bhtd,bhsd->bhts", q[:, :, q0:kend], k[:, :, :kend]).astype(jnp.float32) * scale
        t_idx = q0 + jnp.arange(SB_BLOCK)[:, None]
        s_idx = jnp.arange(kend)[None, :]
        causal = s_idx < t_idx
        log_1mb = jnp.where(causal, jax.nn.log_sigmoid(-z), 0.0)
        tail = lax.cumsum(log_1mb, axis=3, reverse=True) - log_1mb
        att = jnp.where(causal, jnp.exp(jax.nn.log_sigmoid(z) + tail), 0.0)
        outs.append(jnp.einsum("bhts,bhsd->bhtd", att.astype(v.dtype), v[:, :, :kend]))
    return jnp.concatenate(outs, axis=2)


def memory_attention(q_mem, mem_k, mem_v):
    q = split_heads(q_mem, MEM_HEADS)
    k = split_heads(mem_k, MEM_HEADS)
    v = split_heads(mem_v, MEM_HEADS)
    s = jnp.einsum("bhtd,bhmd->bhtm", q, k).astype(jnp.float32) * MEM_DH ** -0.5
    p = jax.nn.softmax(s, axis=-1).astype(v.dtype)
    return merge_heads(jnp.einsum("bhtm,bhmd->bhtd", p, v))


def gdn_branch(proj, conv_w, a_log, dt_bias, out_norm):
    qkv, z, a, b, q_mem = jnp.split(proj, [GDN_QKV, GDN_QKV + GDN_VW, GDN_QKV + GDN_VW + GDN_HEADS,
                                           GDN_QKV + GDN_VW + 2 * GDN_HEADS], axis=-1)
    qkv = jax.nn.silu(causal_dwconv(qkv, conv_w))
    q, k, v = jnp.split(qkv, [GDN_KW, 2 * GDN_KW], axis=-1)
    o = gated_delta_rule(split_heads(q, GDN_HEADS), split_heads(k, GDN_HEADS), split_heads(v, GDN_HEADS),
                         a.transpose(0, 2, 1), b.transpose(0, 2, 1), a_log, dt_bias)
    o = rms_norm(o, out_norm) * jax.nn.silu(split_heads(z, GDN_HEADS).astype(jnp.float32))
    return merge_heads(o).astype(proj.dtype), q_mem


def mlstm_branch(proj, i_bias, f_bias, out_norm):
    q, k, v, o, i_pre, f_pre, q_mem = jnp.split(
        proj, [ML_KW, 2 * ML_KW, 2 * ML_KW + ML_VW, 2 * ML_KW + 2 * ML_VW,
               2 * ML_KW + 2 * ML_VW + ML_HEADS, 2 * ML_KW + 2 * ML_VW + 2 * ML_HEADS], axis=-1)
    i_pre = (i_pre + i_bias).transpose(0, 2, 1)
    f_pre = (f_pre + f_bias).transpose(0, 2, 1)
    h = mlstm_cell(split_heads(q, ML_HEADS), split_heads(k, ML_HEADS), split_heads(v, ML_HEADS), i_pre, f_pre)
    h = rms_norm(h, out_norm) * jax.nn.sigmoid(split_heads(o, ML_HEADS).astype(jnp.float32))
    return merge_heads(h).astype(proj.dtype), q_mem


def sb_branch(proj):
    q, k, v, q_mem = jnp.split(proj, [SB_W, 2 * SB_W, 3 * SB_W], axis=-1)
    o = stick_breaking(split_heads(q, SB_HEADS), split_heads(k, SB_HEADS), split_heads(v, SB_HEADS))
    return merge_heads(o), q_mem


def swiglu(x, w_in, w_out):
    g, u = jnp.split(x @ w_in, 2, axis=-1)
    return (jax.nn.silu(g) * u) @ w_out


def _w(key, shape, fan_in):
    return jax.random.normal(key, shape, jnp.float32) * fan_in ** -0.5


def _gain(key, shape):
    return 1.0 + 0.02 * jax.random.normal(key, shape, jnp.float32)


def setup_inputs(seed: int = 0) -> dict:
    key = jax.random.key(seed)
    ks = jax.random.split(key, 24)
    dt = jnp.exp(jax.random.uniform(ks[14], (N_GDN, GDN_HEADS), jnp.float32,
                                    minval=math.log(1e-3), maxval=math.log(1e-1)))
    return {
        "x": jax.random.normal(ks[0], (BATCH, SEQ, D_MODEL), jnp.float32),
        "mem": jax.random.normal(ks[1], (BATCH, N_MEM, D_MODEL), jnp.float32),
        "mem_norm": _gain(ks[2], (D_MODEL,)),
        "norm_pre_mix": _gain(ks[3], (DEPTH, D_MODEL)),
        "norm_post_mix": _gain(ks[4], (DEPTH, D_MODEL)),
        "norm_pre_ffn": _gain(ks[5], (DEPTH, D_MODEL)),
        "norm_post_ffn": _gain(ks[6], (DEPTH, D_MODEL)),
        "w_mem_kv": _w(ks[7], (DEPTH, D_MODEL, 2 * MEM_W), D_MODEL),
        "w_out": _w(ks[8], (DEPTH, MIX_W, D_MODEL), MIX_W),
        "w_ffn_in": _w(ks[9], (DEPTH, D_MODEL, 2 * D_FF), D_MODEL),
        "w_ffn_out": _w(ks[10], (DEPTH, D_FF, D_MODEL), D_FF),
        "gdn_w_in": _w(ks[11], (N_GDN, D_MODEL, GDN_IN), D_MODEL),
        "gdn_conv": _w(ks[12], (N_GDN, CONV_K, GDN_QKV), CONV_K),
        "gdn_a_log": jnp.log(jax.random.uniform(ks[13], (N_GDN, GDN_HEADS), jnp.float32, minval=1.0, maxval=16.0)),
        "gdn_dt_bias": dt + jnp.log(-jnp.expm1(-dt)),
        "gdn_out_norm": _gain(ks[15], (N_GDN, GDN_DV)),
        "ml_w_in": _w(ks[16], (N_ML, D_MODEL, ML_IN), D_MODEL),
        "ml_i_bias": 0.1 * jax.random.normal(ks[17], (N_ML, ML_HEADS), jnp.float32),
        "ml_f_bias": jax.random.uniform(ks[18], (N_ML, ML_HEADS), jnp.float32, minval=3.0, maxval=6.0),
        "ml_out_norm": _gain(ks[19], (N_ML, ML_DV)),
        "sb_w_in": _w(ks[20], (N_SB, D_MODEL, SB_IN), D_MODEL),
    }


def reference(x, mem, mem_norm, norm_pre_mix, norm_post_mix, norm_pre_ffn, norm_post_ffn, w_mem_kv,
              w_out, w_ffn_in, w_ffn_out, gdn_w_in, gdn_conv, gdn_a_log, gdn_dt_bias, gdn_out_norm,
              ml_w_in, ml_i_bias, ml_f_bias, ml_out_norm, sb_w_in):
    h = x
    mem_n = rms_norm(mem, mem_norm)
    for layer in range(DEPTH):
        kind = layer % N_MIXERS
        j = layer // N_MIXERS
        u = rms_norm(h, norm_pre_mix[layer])
        if kind == 0:
            y_mix, q_mem = gdn_branch(u @ gdn_w_in[j], gdn_conv[j], gdn_a_log[j], gdn_dt_bias[j], gdn_out_norm[j])
        elif kind == 1:
            y_mix, q_mem = mlstm_branch(u @ ml_w_in[j], ml_i_bias[j], ml_f_bias[j], ml_out_norm[j])
        else:
            y_mix, q_mem = sb_branch(u @ sb_w_in[j])
        mem_k, mem_v = jnp.split(mem_n @ w_mem_kv[layer], 2, axis=-1)
        y_mem = memory_attention(q_mem, mem_k, mem_v)
        y = jnp.concatenate([y_mix, y_mem], axis=-1) @ w_out[layer]
        h = h + rms_norm(y, norm_post_mix[layer])
        f = swiglu(rms_norm(h, norm_pre_ffn[layer]), w_ffn_in[layer], w_ffn_out[layer])
        h = h + rms_norm(f, norm_post_ffn[layer])
    return h
```

```python
import functools

import jax
import jax.numpy as jnp
from jax import lax
from jax.experimental import pallas as pl
from jax.experimental.pallas import tpu as pltpu

F32 = jnp.float32
BF16 = jnp.bfloat16
EPS = 1e-6
HI = lax.Precision.HIGHEST

LANES = 128
VMEM_LIMIT_BYTES = 56 * 1024 * 1024

CHUNK = 64
GDN_HEADS = 8
GDN_DK = 128
ML_HEADS = 8
ML_DQK = 64
ML_DV = 128
SB_HEADS = 16
SB_DH = 64
SB_BLOCK = 128
MEM_HEADS = 4
MEM_DH = 128
MEM_W = MEM_HEADS * MEM_DH

_NT = (((1,), (1,)), ((), ()))
_TN = (((0,), (0,)), ((), ()))


def _params(*sem):
    return pltpu.CompilerParams(dimension_semantics=sem, vmem_limit_bytes=VMEM_LIMIT_BYTES)


def _rms(x, g):
    return x * lax.rsqrt(jnp.mean(x * x, axis=-1, keepdims=True) + EPS) * g


def _softplus(x):
    return jnp.maximum(x, 0.0) + jnp.log1p(jnp.exp(-jnp.abs(x)))


def _sigmoid(x):
    return 1.0 / (1.0 + jnp.exp(-x))


def _bdot(a, b):
    return jnp.dot(a.astype(BF16), b.astype(BF16), preferred_element_type=F32)


def _bdot_nt(a, b):
    return lax.dot_general(a.astype(BF16), b.astype(BF16), _NT, preferred_element_type=F32)


def _bdot_tn(a, b):
    return lax.dot_general(a.astype(BF16), b.astype(BF16), _TN, preferred_element_type=F32)


def _xdot(a, b):
    return jnp.dot(a, b, precision=HI, preferred_element_type=F32)


def _norm_matmul_kernel(x_ref, g_ref, w_ref, o_ref, *, col_chunk):
    xn = _rms(x_ref[...], g_ref[...]).astype(BF16)
    n = o_ref.shape[1]
    for c0 in range(0, n, col_chunk):
        c1 = min(c0 + col_chunk, n)
        o_ref[:, c0:c1] = jnp.dot(xn, w_ref[:, c0:c1], preferred_element_type=F32)


def _norm_matmul(x2d, g, w, *, tm, col_chunk=512):
    t, d = x2d.shape
    n = w.shape[1]
    return pl.pallas_call(
        functools.partial(_norm_matmul_kernel, col_chunk=col_chunk),
        grid=(t // tm,),
        in_specs=[pl.BlockSpec((tm, d), lambda i: (i, 0)),
                  pl.BlockSpec((1, d), lambda i: (0, 0)),
                  pl.BlockSpec((d, n), lambda i: (0, 0))],
        out_specs=pl.BlockSpec((tm, n), lambda i: (i, 0)),
        out_shape=jax.ShapeDtypeStruct((t, n), F32),
        compiler_params=_params("arbitrary"),
        name="norm_matmul",
    )(x2d, g.reshape(1, d), w)


def _mem_attn_kernel(q_ref, k_ref, v_ref, o_ref):
    scale = MEM_DH ** -0.5
    for h in range(MEM_HEADS):
        sl = slice(h * MEM_DH, (h + 1) * MEM_DH)
        s = _bdot_nt(q_ref[:, sl], k_ref[:, sl]) * scale
        s = s - jnp.max(s, axis=-1, keepdims=True)
        e = jnp.exp(s)
        p = e / jnp.sum(e, axis=-1, keepdims=True)
        o_ref[:, sl] = _bdot(p, v_ref[:, sl])


def _mem_attention(proj, kv, *, batch, seq, n_mem, qcol, tq):
    t = proj.shape[0]
    nq = seq // tq
    qblk = qcol // MEM_W
    return pl.pallas_call(
        _mem_attn_kernel,
        grid=(batch, nq),
        in_specs=[pl.BlockSpec((tq, MEM_W), lambda b, i: (b * nq + i, qblk)),
                  pl.BlockSpec((n_mem, MEM_W), lambda b, i: (b, 0)),
                  pl.BlockSpec((n_mem, MEM_W), lambda b, i: (b, 1))],
        out_specs=pl.BlockSpec((tq, MEM_W), lambda b, i: (b * nq + i, 0)),
        out_shape=jax.ShapeDtypeStruct((t, MEM_W), F32),
        compiler_params=_params("arbitrary", "arbitrary"),
        name="mem_attention",
    )(proj, kv, kv)


def _tail_kernel(h_ref, ymix_ref, ymem_ref, wo1_ref, wo2_ref, gpm_ref, gpf_ref, win_ref, wout_ref,
                 gqf_ref, o_ref, *, d_ff, ff_chunk):
    y = _bdot(ymix_ref[...], wo1_ref[...]) + _bdot(ymem_ref[...], wo2_ref[...])
    h1 = h_ref[...] + _rms(y, gpm_ref[...])
    n = _rms(h1, gpf_ref[...]).astype(BF16)
    acc = jnp.zeros_like(h1)
    for c0 in range(0, d_ff, ff_chunk):
        g = jnp.dot(n, win_ref[:, c0:c0 + ff_chunk], preferred_element_type=F32)
        u = jnp.dot(n, win_ref[:, d_ff + c0:d_ff + c0 + ff_chunk], preferred_element_type=F32)
        a = (g * _sigmoid(g) * u).astype(BF16)
        acc = acc + jnp.dot(a, wout_ref[c0:c0 + ff_chunk, :], preferred_element_type=F32)
    o_ref[...] = h1 + _rms(acc, gqf_ref[...])


def _layer_tail(h2d, ymix, ymem, wo, g_post_mix, g_pre_ffn, w_in, w_out, g_post_ffn, *, tm, ff_chunk):
    t, d = h2d.shape
    mixw = ymix.shape[1]
    d_ff = w_out.shape[0]
    row = lambda i: (i, 0)
    fixed = lambda i: (0, 0)
    return pl.pallas_call(
        functools.partial(_tail_kernel, d_ff=d_ff, ff_chunk=ff_chunk),
        grid=(t // tm,),
        in_specs=[pl.BlockSpec((tm, d), row),
                  pl.BlockSpec((tm, mixw), row),
                  pl.BlockSpec((tm, MEM_W), row),
                  pl.BlockSpec((mixw, d), fixed),
                  pl.BlockSpec((MEM_W, d), fixed),
                  pl.BlockSpec((1, d), fixed),
                  pl.BlockSpec((1, d), fixed),
                  pl.BlockSpec((d, 2 * d_ff), fixed),
                  pl.BlockSpec((d_ff, d), fixed),
                  pl.BlockSpec((1, d), fixed)],
        out_specs=pl.BlockSpec((tm, d), row),
        out_shape=jax.ShapeDtypeStruct((t, d), F32),
        compiler_params=_params("arbitrary"),
        name="layer_tail",
    )(h2d, ymix, ymem, wo[:mixw], wo[mixw:], g_post_mix.reshape(1, d), g_pre_ffn.reshape(1, d),
      w_in, w_out, g_post_ffn.reshape(1, d))


def _select_lane(x, lane):
    r = lax.broadcasted_iota(jnp.int32, (LANES, LANES), 0)
    return _xdot(x, (r == lane).astype(F32))


def _tri(n, lower, strict):
    r = lax.broadcasted_iota(jnp.int32, (n, n), 0)
    c = lax.broadcasted_iota(jnp.int32, (n, n), 1)
    if lower:
        return (r > c) if strict else (r >= c)
    return (r < c) if strict else (r <= c)


def _gdn_kernel(q_ref, k_ref, v_ref, z_ref, ab_ref, at_ref, wq_ref, wk_ref, wv_ref, alog_ref, dtb_ref,
                onorm_ref, o_ref, g_sc, beta_sc, grow_sc, *, seq):
    c = CHUNK
    nchunk = seq // c
    head = pl.program_id(1)
    neg_a = -jnp.exp(alog_ref[pl.ds(head, 1), :])
    dtb = dtb_ref[pl.ds(head, 1), :]
    ab = ab_ref[...]
    g_sc[...] = neg_a * _softplus(_select_lane(ab, head) + dtb)
    beta_sc[...] = _sigmoid(_select_lane(ab, GDN_HEADS + head))
    g_row = neg_a[:, :c] * _softplus(at_ref[0, 0] + dtb[:, :c])
    grow_sc[...] = _xdot(g_row, _tri(c, lower=False, strict=False).astype(F32))

    incl = _tri(c, lower=True, strict=False)
    strict = _tri(c, lower=True, strict=True)
    tril_f = incl.astype(F32)
    eye_f = (lax.broadcasted_iota(jnp.int32, (c, c), 0) == lax.broadcasted_iota(jnp.int32, (c, c), 1)).astype(F32)
    rxc = lax.broadcasted_iota(jnp.int32, (c, c), 0) ^ lax.broadcasted_iota(jnp.int32, (c, c), 1)
    onorm = onorm_ref[...]
    wq = wq_ref[...]
    wk = wk_ref[...]
    wv = wv_ref[...]

    def conv_silu(x_ref, w, t0, first):
        cur = x_ref[pl.ds(t0, c), :]
        prev = x_ref[pl.ds(jnp.maximum(t0 - 8, 0), 8), :] * jnp.where(first, 0.0, 1.0)
        xe = jnp.concatenate([prev, cur], axis=0)
        y = (w[0:1] * xe[5:5 + c] + w[1:2] * xe[6:6 + c] + w[2:3] * xe[7:7 + c] + w[3:4] * cur)
        return y * _sigmoid(y)

    def l2n(x):
        return x * lax.rsqrt(jnp.sum(x * x, axis=-1, keepdims=True) + EPS)

    def body(n, state):
        t0 = pl.multiple_of(n * c, c)
        first = n == 0
        qc = l2n(conv_silu(q_ref, wq, t0, first)) * (GDN_DK ** -0.5)
        kc = l2n(conv_silu(k_ref, wk, t0, first))
        vc = conv_silu(v_ref, wv, t0, first)
        beta = beta_sc[pl.ds(t0, c), :]
        gcum = _xdot(tril_f, g_sc[pl.ds(t0, c), :])
        g_row = grow_sc[pl.ds(n, 1), :]
        diff = gcum[:, :c] - g_row
        decay = jnp.where(incl, jnp.exp(jnp.where(incl, diff, 0.0)), 0.0)
        kk = _bdot_nt(kc, kc)
        a_mat = jnp.where(strict, beta[:, :c] * kk * decay, 0.0)
        x = eye_f - jnp.where(rxc == 1, a_mat, 0.0)
        for lvl in range(1, 6):
            a_l = jnp.where(lax.shift_right_logical(rxc, lvl) == 1, a_mat, 0.0)
            x = x - _xdot(_xdot(x, a_l), x)
        eg = jnp.exp(gcum)
        u = _bdot(x, vc * beta)
        w = _bdot(x, kc * (beta * eg))
        p = _bdot_nt(qc, kc) * decay
        g_last = gcum[c - 1:c, :]
        qg = qc * eg
        kg = kc * jnp.exp(g_last - gcum)
        v_new = u - _bdot(w, state)
        o = _bdot(qg, state) + _bdot(p, v_new)
        state = state * jnp.exp(g_last) + _bdot_tn(kg, v_new)
        z = z_ref[pl.ds(t0, c), :]
        o_ref[pl.ds(t0, c), :] = _rms(o, onorm) * (z * _sigmoid(z))
        return state

    lax.fori_loop(0, nchunk, body, jnp.zeros((GDN_DK, LANES), F32))


def _gdn_mixer(proj, conv_w, a_log, dt_bias, out_norm, *, batch, seq, ab_col):
    t = proj.shape[0]
    c = CHUNK
    nchunk = seq // c
    hh = GDN_HEADS
    a_t = proj[:, ab_col:ab_col + hh].reshape(batch, seq, hh).transpose(0, 2, 1).reshape(batch, hh, nchunk, c)
    bcast = lambda v: jnp.broadcast_to(v.astype(F32)[:, None], (hh, LANES))
    col = lambda off: (lambda b, h: (b, off + h))
    wcol = lambda off: (lambda b, h: (0, off + h))
    fixed = lambda b, h: (0, 0)
    return pl.pallas_call(
        functools.partial(_gdn_kernel, seq=seq),
        grid=(batch, hh),
        in_specs=[pl.BlockSpec((seq, LANES), col(0)),
                  pl.BlockSpec((seq, LANES), col(hh)),
                  pl.BlockSpec((seq, LANES), col(2 * hh)),
                  pl.BlockSpec((seq, LANES), col(3 * hh)),
                  pl.BlockSpec((seq, LANES), lambda b, h: (b, ab_col // LANES)),
                  pl.BlockSpec((1, 1, nchunk, c), lambda b, h: (b, h, 0, 0)),
                  pl.BlockSpec((4, LANES), wcol(0)),
                  pl.BlockSpec((4, LANES), wcol(hh)),
                  pl.BlockSpec((4, LANES), wcol(2 * hh)),
                  pl.BlockSpec((hh, LANES), fixed),
                  pl.BlockSpec((hh, LANES), fixed),
                  pl.BlockSpec((1, LANES), fixed)],
        out_specs=pl.BlockSpec((seq, LANES), col(0)),
        out_shape=jax.ShapeDtypeStruct((t, hh * LANES), F32),
        scratch_shapes=[pltpu.VMEM((seq, LANES), F32),
                        pltpu.VMEM((seq, LANES), F32),
                        pltpu.VMEM((nchunk, c), F32)],
        compiler_params=_params("arbitrary", "arbitrary"),
        name="gdn_mixer",
    )(proj, proj, proj, proj, proj, a_t, conv_w, conv_w, conv_w, bcast(a_log), bcast(dt_bias),
      out_norm.reshape(1, LANES))


def _mlstm_kernel(q_ref, k_ref, v_ref, og_ref, if_ref, it_ref, ft_ref, ib_ref, fb_ref, onorm_ref, o_ref,
                  li_sc, lf_sc, lirow_sc, bcrow_sc, *, seq):
    c = CHUNK
    nchunk = seq // c
    pair = pl.program_id(1)
    gates = if_ref[...]
    triu_f = _tri(c, lower=False, strict=False).astype(F32)
    for j in range(2):
        head = 2 * pair + j
        ib = ib_ref[pl.ds(head, 1), :]
        fb = fb_ref[pl.ds(head, 1), :]
        li_sc[j] = _select_lane(gates, head) + ib
        lf_sc[j] = -_softplus(-(_select_lane(gates, ML_HEADS + head) + fb))
        lirow_sc[j] = it_ref[0, j] + ib[:, :c]
        bcrow_sc[j] = _xdot(-_softplus(-(ft_ref[0, j] + fb[:, :c])), triu_f)

    incl = _tri(c, lower=True, strict=False)
    tril_f = incl.astype(F32)
    lane = lax.broadcasted_iota(jnp.int32, (1, LANES), 1)
    onorm = onorm_ref[...]
    ones_v = jnp.ones((c, ML_DV), F32)

    def body(n, carry):
        t0 = pl.multiple_of(n * c, c)
        qp = q_ref[pl.ds(t0, c), :]
        kp = k_ref[pl.ds(t0, c), :] * (ML_DQK ** -0.5)
        new_carry = []
        for j in range(2):
            cst, mst = carry[2 * j], carry[2 * j + 1]
            hm = (lane >= j * ML_DQK) & (lane < (j + 1) * ML_DQK)
            qj = jnp.where(hm, qp, 0.0)
            kj = jnp.where(hm, kp, 0.0)
            vj = v_ref[pl.ds(t0, c), j * ML_DV:(j + 1) * ML_DV]
            va = jnp.concatenate([vj, ones_v], axis=1)
            li = li_sc[j, pl.ds(t0, c), :]
            bcum = _xdot(tril_f, lf_sc[j, pl.ds(t0, c), :])
            bc_row = bcrow_sc[j, pl.ds(n, 1), :]
            li_row = lirow_sc[j, pl.ds(n, 1), :]
            dmat = jnp.where(incl, bcum[:, :c] - bc_row + li_row, -jnp.inf)
            m_intra = jnp.max(dmat, axis=-1, keepdims=True)
            sqk = _bdot_nt(qj, kj) * jnp.exp(dmat - m_intra)
            intra = _bdot(sqk, va)
            a_inter = bcum + mst
            m_t = jnp.maximum(a_inter, m_intra)
            s_inter = jnp.exp(a_inter - m_t)
            s_intra = jnp.exp(m_intra - m_t)
            inter = _bdot(qj, cst)
            num = s_inter * inter[:, :ML_DV] + s_intra * intra[:, :ML_DV]
            den = s_inter * inter[:, ML_DV:] + s_intra * intra[:, ML_DV:]
            hout = num / jnp.maximum(jnp.abs(den), jnp.exp(-m_t))
            og = og_ref[pl.ds(t0, c), j * ML_DV:(j + 1) * ML_DV]
            o_ref[pl.ds(t0, c), j * ML_DV:(j + 1) * ML_DV] = _rms(hout, onorm) * _sigmoid(og)
            bl = bcum[c - 1:c, :]
            wk = bl - bcum + li
            m_chunk = jnp.max(wk, axis=0, keepdims=True)
            e = jnp.exp(wk - m_chunk)
            kv = _bdot_tn(kj * e, va)
            m_new = jnp.maximum(bl + mst, m_chunk)
            fa = jnp.exp(bl + mst - m_new)
            fc = jnp.exp(m_chunk - m_new)
            cst = jnp.concatenate([fa, fa], axis=1) * cst + jnp.concatenate([fc, fc], axis=1) * kv
            new_carry += [cst, m_new]
        return tuple(new_carry)

    init = (jnp.zeros((LANES, 2 * ML_DV), F32), jnp.zeros((1, LANES), F32)) * 2
    lax.fori_loop(0, nchunk, body, init)


def _mlstm_mixer(proj, i_bias, f_bias, out_norm, *, batch, seq, if_col):
    t = proj.shape[0]
    c = CHUNK
    nchunk = seq // c
    hh = ML_HEADS
    npair = hh // 2
    kw = hh * ML_DQK
    vw = hh * ML_DV
    rows = lambda off: proj[:, off:off + hh].reshape(batch, seq, hh).transpose(0, 2, 1).reshape(batch, hh, nchunk, c)
    bcast = lambda v: jnp.broadcast_to(v.astype(F32)[:, None], (hh, LANES))
    fixed = lambda b, p: (0, 0)
    pw = 2 * ML_DV
    return pl.pallas_call(
        functools.partial(_mlstm_kernel, seq=seq),
        grid=(batch, npair),
        in_specs=[pl.BlockSpec((seq, LANES), lambda b, p: (b, p)),
                  pl.BlockSpec((seq, LANES), lambda b, p: (b, kw // LANES + p)),
                  pl.BlockSpec((seq, pw), lambda b, p: (b, 2 * kw // pw + p)),
                  pl.BlockSpec((seq, pw), lambda b, p: (b, (2 * kw + vw) // pw + p)),
                  pl.BlockSpec((seq, LANES), lambda b, p: (b, if_col // LANES)),
                  pl.BlockSpec((1, 2, nchunk, c), lambda b, p: (b, p, 0, 0)),
                  pl.BlockSpec((1, 2, nchunk, c), lambda b, p: (b, p, 0, 0)),
                  pl.BlockSpec((hh, LANES), fixed),
                  pl.BlockSpec((hh, LANES), fixed),
                  pl.BlockSpec((1, LANES), fixed)],
        out_specs=pl.BlockSpec((seq, pw), lambda b, p: (b, p)),
        out_shape=jax.ShapeDtypeStruct((t, vw), F32),
        scratch_shapes=[pltpu.VMEM((2, seq, LANES), F32),
                        pltpu.VMEM((2, seq, LANES), F32),
                        pltpu.VMEM((2, nchunk, c), F32),
                        pltpu.VMEM((2, nchunk, c), F32)],
        compiler_params=_params("arbitrary", "arbitrary"),
        name="mlstm_mixer",
    )(proj, proj, proj, proj, proj, rows(if_col), rows(if_col + hh), bcast(i_bias), bcast(f_bias),
      out_norm.reshape(1, LANES))


def _sb_kernel(q_ref, k_ref, v_ref, o_ref, *, seq):
    blk = SB_BLOCK
    nblk = seq // blk
    scale = SB_DH ** -0.5
    lane = lax.broadcasted_iota(jnp.int32, (1, LANES), 1)
    r = lax.broadcasted_iota(jnp.int32, (blk, blk), 0)
    cc = lax.broadcasted_iota(jnp.int32, (blk, blk), 1)
    causal = cc < r
    m_suffix = (r > cc).astype(BF16)
    m_half = jnp.concatenate([m_suffix, jnp.ones((blk, blk), BF16)], axis=1)
    m_cat = jnp.concatenate([m_half, m_half], axis=0)

    def block(qm, kb, carry, acc, diagonal):
        s0 = pl.multiple_of(kb * blk, blk)
        z = _bdot_nt(qm, k_ref[pl.ds(s0, blk), :])
        lg = -_softplus(z)
        if diagonal:
            lg = jnp.where(causal, lg, 0.0)
        hi = lg.astype(BF16)
        lo = (lg - hi.astype(F32)).astype(BF16)
        sums = jnp.dot(jnp.concatenate([hi, lo], axis=1), m_cat, preferred_element_type=F32)
        att = jnp.exp(z + lg + carry + sums[:, :blk])
        if diagonal:
            att = jnp.where(causal, att, 0.0)
        acc = acc + _bdot(att, v_ref[pl.ds(s0, blk), :])
        return carry + sums[:, blk:], acc

    def q_block(qb, _):
        t0 = pl.multiple_of(qb * blk, blk)
        qp = q_ref[pl.ds(t0, blk), :] * scale
        out = jnp.zeros((blk, LANES), F32)
        for j in range(2):
            hm = (lane >= j * SB_DH) & (lane < (j + 1) * SB_DH)
            qm = jnp.where(hm, qp, 0.0).astype(BF16)
            zero = jnp.zeros((blk, blk), F32)
            carry, acc = block(qm, qb, zero, zero, True)

            def inner(i, ca):
                return block(qm, qb - 1 - i, ca[0], ca[1], False)

            carry, acc = lax.fori_loop(0, qb, inner, (carry, acc))
            out = jnp.where(hm, acc, out)
        o_ref[pl.ds(t0, blk), :] = out
        return 0

    lax.fori_loop(0, nblk, q_block, 0)


def _sb_mixer(proj, *, batch, seq):
    t = proj.shape[0]
    npair = SB_HEADS // 2
    w = SB_HEADS * SB_DH
    return pl.pallas_call(
        functools.partial(_sb_kernel, seq=seq),
        grid=(batch, npair),
        in_specs=[pl.BlockSpec((seq, LANES), lambda b, p: (b, p)),
                  pl.BlockSpec((seq, LANES), lambda b, p: (b, w // LANES + p)),
                  pl.BlockSpec((seq, LANES), lambda b, p: (b, 2 * w // LANES + p))],
        out_specs=pl.BlockSpec((seq, LANES), lambda b, p: (b, p)),
        out_shape=jax.ShapeDtypeStruct((t, w), F32),
        compiler_params=_params("arbitrary", "arbitrary"),
        name="sb_mixer",
    )(proj, proj, proj)


def _pad_cols(w, width):
    return jnp.pad(w, ((0, 0), (0, width - w.shape[1])))


def kernel(x, mem, mem_norm, norm_pre_mix, norm_post_mix, norm_pre_ffn, norm_post_ffn, w_mem_kv, w_out,
           w_ffn_in, w_ffn_out, gdn_w_in, gdn_conv, gdn_a_log, gdn_dt_bias, gdn_out_norm, ml_w_in, ml_i_bias,
           ml_f_bias, ml_out_norm, sb_w_in):
    batch, seq, d = x.shape
    n_mem = mem.shape[1]
    depth = w_out.shape[0]
    d_ff = w_ffn_out.shape[1]
    h = x.reshape(batch * seq, d)
    mem2d = mem.reshape(batch * n_mem, d)
    tm = 512

    gdn_main = 3 * GDN_HEADS * GDN_DK + GDN_HEADS * LANES
    ml_main = 2 * ML_HEADS * ML_DQK + 2 * ML_HEADS * ML_DV
    sb_main = 3 * SB_HEADS * SB_DH

    for layer in range(depth):
        kind, j = layer % 3, layer // 3
        if kind == 0:
            w = gdn_w_in[j]
            gates = w[:, gdn_main:gdn_main + 2 * GDN_HEADS]
            w = jnp.concatenate([w[:, :gdn_main], w[:, gdn_main + 2 * GDN_HEADS:], _pad_cols(gates, LANES)], axis=1)
            main = gdn_main
        elif kind == 1:
            w = ml_w_in[j]
            gates = w[:, ml_main:ml_main + 2 * ML_HEADS]
            w = jnp.concatenate([w[:, :ml_main], w[:, ml_main + 2 * ML_HEADS:], _pad_cols(gates, LANES)], axis=1)
            main = ml_main
        else:
            w = sb_w_in[j]
            main = sb_main
        proj = _norm_matmul(h, norm_pre_mix[layer], w.astype(BF16), tm=tm)
        if kind == 0:
            ymix = _gdn_mixer(proj, gdn_conv[j], gdn_a_log[j], gdn_dt_bias[j], gdn_out_norm[j],
                              batch=batch, seq=seq, ab_col=main + MEM_W)
        elif kind == 1:
            ymix = _mlstm_mixer(proj, ml_i_bias[j], ml_f_bias[j], ml_out_norm[j],
                                batch=batch, seq=seq, if_col=main + MEM_W)
        else:
            ymix = _sb_mixer(proj, batch=batch, seq=seq)
        kv = _norm_matmul(mem2d, mem_norm, w_mem_kv[layer].astype(BF16), tm=n_mem)
        ymem = _mem_attention(proj, kv, batch=batch, seq=seq, n_mem=n_mem, qcol=main, tq=512)
        h = _layer_tail(h, ymix, ymem, w_out[layer].astype(BF16), norm_post_mix[layer], norm_pre_ffn[layer],
                        w_ffn_in[layer].astype(BF16), w_ffn_out[layer].astype(BF16), norm_post_ffn[layer],
                        tm=tm, ff_chunk=d_ff // 2)
    return h.reshape(batch, seq, d)
```

```python
import functools

import jax
import jax.numpy as jnp
from jax import lax
from jax.experimental import pallas as pl
from jax.experimental.pallas import tpu as pltpu

F32 = jnp.float32
BF16 = jnp.bfloat16
EPS = 1e-6
HI = lax.Precision.HIGHEST

LANES = 128
VMEM_LIMIT_BYTES = 56 * 1024 * 1024

GDN_CHUNK = 128
ML_CHUNK = 128
GDN_HEADS = 8
GDN_DK = 128
ML_HEADS = 8
ML_DQK = 64
ML_DV = 128
SB_HEADS = 16
SB_DH = 64
SB_BLOCK = 128
MEM_HEADS = 4
MEM_DH = 128
MEM_W = MEM_HEADS * MEM_DH

_NT = (((1,), (1,)), ((), ()))
_TN = (((0,), (0,)), ((), ()))


def _params(*sem):
    return pltpu.CompilerParams(dimension_semantics=sem, vmem_limit_bytes=VMEM_LIMIT_BYTES)


def _rms(x, g):
    return x * lax.rsqrt(jnp.mean(x * x, axis=-1, keepdims=True) + EPS) * g


def _softplus(x):
    return jnp.maximum(x, 0.0) + jnp.log1p(jnp.exp(-jnp.abs(x)))


def _sigmoid(x):
    return 1.0 / (1.0 + jnp.exp(-x))


def _bdot(a, b):
    return jnp.dot(a.astype(BF16), b.astype(BF16), preferred_element_type=F32)


def _bdot_nt(a, b):
    return lax.dot_general(a.astype(BF16), b.astype(BF16), _NT, preferred_element_type=F32)


def _bdot_tn(a, b):
    return lax.dot_general(a.astype(BF16), b.astype(BF16), _TN, preferred_element_type=F32)


def _xdot(a, b):
    return jnp.dot(a, b, precision=HI, preferred_element_type=F32)


def _norm_matmul_kernel(x_ref, g_ref, w_ref, o_ref, *, col_chunk):
    xn = _rms(x_ref[...], g_ref[...]).astype(BF16)
    n = o_ref.shape[1]
    for c0 in range(0, n, col_chunk):
        c1 = min(c0 + col_chunk, n)
        o_ref[:, c0:c1] = jnp.dot(xn, w_ref[:, c0:c1], preferred_element_type=F32)


def _norm_matmul(x2d, g, w, *, tm, col_chunk=512):
    t, d = x2d.shape
    n = w.shape[1]
    return pl.pallas_call(
        functools.partial(_norm_matmul_kernel, col_chunk=col_chunk),
        grid=(t // tm,),
        in_specs=[pl.BlockSpec((tm, d), lambda i: (i, 0)),
                  pl.BlockSpec((1, d), lambda i: (0, 0)),
                  pl.BlockSpec((d, n), lambda i: (0, 0))],
        out_specs=pl.BlockSpec((tm, n), lambda i: (i, 0)),
        out_shape=jax.ShapeDtypeStruct((t, n), F32),
        compiler_params=_params("arbitrary"),
        name="norm_matmul",
    )(x2d, g.reshape(1, d), w)


def _mem_attn_kernel(q_ref, k_ref, v_ref, o_ref):
    scale = MEM_DH ** -0.5
    for h in range(MEM_HEADS):
        sl = slice(h * MEM_DH, (h + 1) * MEM_DH)
        s = _bdot_nt(q_ref[:, sl], k_ref[:, sl]) * scale
        s = s - jnp.max(s, axis=-1, keepdims=True)
        e = jnp.exp(s)
        p = e / jnp.sum(e, axis=-1, keepdims=True)
        o_ref[:, sl] = _bdot(p, v_ref[:, sl])


def _mem_attention(proj, kv, *, batch, seq, n_mem, qcol, tq):
    t = proj.shape[0]
    nq = seq // tq
    qblk = qcol // MEM_W
    return pl.pallas_call(
        _mem_attn_kernel,
        grid=(batch, nq),
        in_specs=[pl.BlockSpec((tq, MEM_W), lambda b, i: (b * nq + i, qblk)),
                  pl.BlockSpec((n_mem, MEM_W), lambda b, i: (b, 0)),
                  pl.BlockSpec((n_mem, MEM_W), lambda b, i: (b, 1))],
        out_specs=pl.BlockSpec((tq, MEM_W), lambda b, i: (b * nq + i, 0)),
        out_shape=jax.ShapeDtypeStruct((t, MEM_W), F32),
        compiler_params=_params("arbitrary", "arbitrary"),
        name="mem_attention",
    )(proj, kv, kv)


def _tail_kernel(h_ref, ymix_ref, ymem_ref, wo1_ref, wo2_ref, gpm_ref, gpf_ref, win_ref, wout_ref,
                 gqf_ref, o_ref, *, d_ff, ff_chunk):
    y = _bdot(ymix_ref[...], wo1_ref[...]) + _bdot(ymem_ref[...], wo2_ref[...])
    h1 = h_ref[...] + _rms(y, gpm_ref[...])
    n = _rms(h1, gpf_ref[...]).astype(BF16)
    acc = jnp.zeros_like(h1)
    for c0 in range(0, d_ff, ff_chunk):
        g = jnp.dot(n, win_ref[:, c0:c0 + ff_chunk], preferred_element_type=F32)
        u = jnp.dot(n, win_ref[:, d_ff + c0:d_ff + c0 + ff_chunk], preferred_element_type=F32)
        a = (g * _sigmoid(g) * u).astype(BF16)
        acc = acc + jnp.dot(a, wout_ref[c0:c0 + ff_chunk, :], preferred_element_type=F32)
    o_ref[...] = h1 + _rms(acc, gqf_ref[...])


def _layer_tail(h2d, ymix, ymem, wo, g_post_mix, g_pre_ffn, w_in, w_out, g_post_ffn, *, tm, ff_chunk):
    t, d = h2d.shape
    mixw = ymix.shape[1]
    d_ff = w_out.shape[0]
    row = lambda i: (i, 0)
    fixed = lambda i: (0, 0)
    return pl.pallas_call(
        functools.partial(_tail_kernel, d_ff=d_ff, ff_chunk=ff_chunk),
        grid=(t // tm,),
        in_specs=[pl.BlockSpec((tm, d), row),
                  pl.BlockSpec((tm, mixw), row),
                  pl.BlockSpec((tm, MEM_W), row),
                  pl.BlockSpec((mixw, d), fixed),
                  pl.BlockSpec((MEM_W, d), fixed),
                  pl.BlockSpec((1, d), fixed),
                  pl.BlockSpec((1, d), fixed),
                  pl.BlockSpec((d, 2 * d_ff), fixed),
                  pl.BlockSpec((d_ff, d), fixed),
                  pl.BlockSpec((1, d), fixed)],
        out_specs=pl.BlockSpec((tm, d), row),
        out_shape=jax.ShapeDtypeStruct((t, d), F32),
        compiler_params=_params("arbitrary"),
        name="layer_tail",
    )(h2d, ymix, ymem, wo[:mixw], wo[mixw:], g_post_mix.reshape(1, d), g_pre_ffn.reshape(1, d),
      w_in, w_out, g_post_ffn.reshape(1, d))


def _select_lane(x, lane):
    r = lax.broadcasted_iota(jnp.int32, (LANES, LANES), 0)
    return _xdot(x, (r == lane).astype(F32))


def _tri(n, lower, strict):
    r = lax.broadcasted_iota(jnp.int32, (n, n), 0)
    c = lax.broadcasted_iota(jnp.int32, (n, n), 1)
    if lower:
        return (r > c) if strict else (r >= c)
    return (r < c) if strict else (r <= c)


def _gdn_kernel(q_ref, k_ref, v_ref, z_ref, ab_ref, at_ref, wq_ref, wk_ref, wv_ref, alane_ref, dlane_ref,
                alog_ref, dtb_ref, onorm_ref, o_ref, gall_sc, beta_sc, grow_sc, u_sc, wq_sc, kg_sc, p_sc, gl_sc,
                *, seq):
    c = GDN_CHUNK
    nchunk = seq // c
    pair = pl.program_id(1)
    lane = lax.broadcasted_iota(jnp.int32, (1, LANES), 1)
    incl = _tri(c, lower=True, strict=False)
    strict = _tri(c, lower=True, strict=True)
    tril_f = incl.astype(F32)
    rxc = lax.broadcasted_iota(jnp.int32, (c, c), 0) ^ lax.broadcasted_iota(jnp.int32, (c, c), 1)
    onorm = onorm_ref[...]

    ab = ab_ref[...]
    beta_sc[...] = _sigmoid(ab)
    gall_sc[...] = -jnp.exp(alane_ref[...]) * _softplus(ab + dlane_ref[...])

    def cum_body(n, _):
        t0 = pl.multiple_of(n * c, c)
        gall_sc[pl.ds(t0, c), :] = _xdot(tril_f, gall_sc[pl.ds(t0, c), :])
        return 0

    lax.fori_loop(0, nchunk, cum_body, 0)
    triu_f = _tri(c, lower=False, strict=False).astype(F32)
    for j in range(2):
        head = 2 * pair + j
        neg_a = -jnp.exp(alog_ref[pl.ds(head, 1), :])
        g_row = neg_a * _softplus(at_ref[0, j] + dtb_ref[pl.ds(head, 1), :])
        grow_sc[j] = _xdot(g_row, triu_f)

    def conv_silu(x_ref, w, t0, first, sl):
        cur = x_ref[pl.ds(t0, c), sl]
        p0 = pl.multiple_of(jnp.maximum(t0 - 8, 0), 8)
        prev = x_ref[pl.ds(p0, 8), sl] * jnp.where(first, 0.0, 1.0)
        xe = jnp.concatenate([prev, cur], axis=0)
        y = (w[0:1] * xe[5:5 + c] + w[1:2] * xe[6:6 + c] + w[2:3] * xe[7:7 + c] + w[3:4] * cur)
        return y * _sigmoid(y)

    def l2n(x):
        return x * lax.rsqrt(jnp.sum(x * x, axis=-1, keepdims=True) + EPS)

    def take_lane(x, idx):
        return jnp.sum(jnp.where(lane == idx, x, 0.0), axis=-1, keepdims=True)

    heads = range(2)

    def prep(n, j):
        t0 = pl.multiple_of(n * c, c)
        first = n == 0
        head = 2 * pair + j
        sl = slice(j * LANES, (j + 1) * LANES)
        qc = l2n(conv_silu(q_ref, wq_ref[:, sl], t0, first, sl)) * (GDN_DK ** -0.5)
        kc = l2n(conv_silu(k_ref, wk_ref[:, sl], t0, first, sl))
        vc = conv_silu(v_ref, wv_ref[:, sl], t0, first, sl)
        gcol = take_lane(gall_sc[pl.ds(t0, c), :], head)
        beta = take_lane(beta_sc[pl.ds(t0, c), :], GDN_HEADS + head)
        g_row = grow_sc[j, pl.ds(n, 1), :]
        decay = jnp.where(incl, jnp.exp(jnp.where(incl, gcol - g_row, 0.0)), 0.0)
        return dict(n=n, j=j, t0=t0, qc=qc, kc=kc, vc=vc, gcol=gcol, beta=beta, decay=decay)

    def pass1(i, _):
        ch = [prep(2 * i + dn, j) for dn in range(2) for j in heads]
        kq = [_bdot_nt(jnp.concatenate([x["kc"], x["qc"]], axis=0), x["kc"]) for x in ch]
        a_mat = [jnp.where(strict, x["beta"] * m[:c] * x["decay"], 0.0) for x, m in zip(ch, kq)]
        nm = [-jnp.where(rxc == 1, a, 0.0) for a in a_mat]
        for lvl in range(1, c.bit_length() - 1):
            mask = lax.shift_right_logical(rxc, lvl) == 1
            a_l = [jnp.where(mask, a, 0.0) for a in a_mat]
            y = [_bdot(m, a) for m, a in zip(nm, a_l)]
            ay = [a + yy for a, yy in zip(a_l, y)]
            z = [_bdot(a, m) for a, m in zip(ay, nm)]
            nm = [m - a - zz for m, a, zz in zip(nm, ay, z)]
        eg = [jnp.exp(x["gcol"]) for x in ch]
        rhs = [jnp.concatenate([x["vc"] * x["beta"], x["kc"] * (x["beta"] * e)], axis=1) for x, e in zip(ch, eg)]
        sol = [r + _bdot(m, r) for r, m in zip(rhs, nm)]
        for x, s, e, m in zip(ch, sol, eg, kq):
            j, n, t0 = x["j"], x["n"], x["t0"]
            g_last = x["gcol"][c - 1:c, :]
            u_sc[j, pl.ds(t0, c), :] = s[:, :LANES]
            wq_sc[j, n, 0:c, :] = s[:, LANES:].astype(BF16)
            wq_sc[j, n, c:2 * c, :] = (x["qc"] * e).astype(BF16)
            kg_sc[j, pl.ds(t0, c), :] = (x["kc"] * jnp.exp(g_last - x["gcol"])).astype(BF16)
            p_sc[j, pl.ds(t0, c), :] = (m[c:] * x["decay"]).astype(BF16)
            gl_sc[j, pl.ds(n, 1), :] = jnp.broadcast_to(jnp.exp(g_last), (1, LANES))
        return 0

    lax.fori_loop(0, nchunk // 2, pass1, 0)

    def pass2(n, states):
        t0 = pl.multiple_of(n * c, c)
        ws = [_bdot(wq_sc[j, n], states[j]) for j in heads]
        v_new = [u_sc[j, pl.ds(t0, c), :] - ws[j][:c] for j in heads]
        upd = [_bdot_tn(kg_sc[j, pl.ds(t0, c), :], v_new[j]) for j in heads]
        o = [ws[j][c:] + _bdot(p_sc[j, pl.ds(t0, c), :], v_new[j]) for j in heads]
        for j in heads:
            sl = slice(j * LANES, (j + 1) * LANES)
            z = z_ref[pl.ds(t0, c), sl]
            o_ref[pl.ds(t0, c), sl] = _rms(o[j], onorm) * (z * _sigmoid(z))
        return tuple(states[j] * gl_sc[j, pl.ds(n, 1), :] + upd[j] for j in heads)

    zero = jnp.zeros((GDN_DK, LANES), F32)
    lax.fori_loop(0, nchunk, pass2, (zero, zero))


def _gdn_mixer(proj, conv_w, a_log, dt_bias, out_norm, *, batch, seq, ab_col):
    t = proj.shape[0]
    c = GDN_CHUNK
    nchunk = seq // c
    hh = GDN_HEADS
    npair = hh // 2
    pw = 2 * LANES
    a_t = proj[:, ab_col:ab_col + hh].reshape(batch, seq, hh).transpose(0, 2, 1).reshape(batch, hh, nchunk, c)
    bcast = lambda v: jnp.broadcast_to(v.astype(F32)[:, None], (hh, LANES))
    lanes = lambda v: jnp.pad(v.astype(F32), (0, LANES - hh)).reshape(1, LANES)
    col = lambda off: (lambda b, p: (b, off + p))
    wcol = lambda off: (lambda b, p: (0, off + p))
    fixed = lambda b, p: (0, 0)
    return pl.pallas_call(
        functools.partial(_gdn_kernel, seq=seq),
        grid=(batch, npair),
        in_specs=[pl.BlockSpec((seq, pw), col(0)),
                  pl.BlockSpec((seq, pw), col(npair)),
                  pl.BlockSpec((seq, pw), col(2 * npair)),
                  pl.BlockSpec((seq, pw), col(3 * npair)),
                  pl.BlockSpec((seq, LANES), lambda b, p: (b, ab_col // LANES)),
                  pl.BlockSpec((1, 2, nchunk, c), lambda b, p: (b, p, 0, 0)),
                  pl.BlockSpec((4, pw), wcol(0)),
                  pl.BlockSpec((4, pw), wcol(npair)),
                  pl.BlockSpec((4, pw), wcol(2 * npair)),
                  pl.BlockSpec((1, LANES), fixed),
                  pl.BlockSpec((1, LANES), fixed),
                  pl.BlockSpec((hh, LANES), fixed),
                  pl.BlockSpec((hh, LANES), fixed),
                  pl.BlockSpec((1, LANES), fixed)],
        out_specs=pl.BlockSpec((seq, pw), col(0)),
        out_shape=jax.ShapeDtypeStruct((t, hh * LANES), F32),
        scratch_shapes=[pltpu.VMEM((seq, LANES), F32),
                        pltpu.VMEM((seq, LANES), F32),
                        pltpu.VMEM((2, nchunk, c), F32),
                        pltpu.VMEM((2, seq, LANES), F32),
                        pltpu.VMEM((2, nchunk, 2 * c, LANES), BF16),
                        pltpu.VMEM((2, seq, LANES), BF16),
                        pltpu.VMEM((2, seq, c), BF16),
                        pltpu.VMEM((2, nchunk, LANES), F32)],
        compiler_params=_params("arbitrary", "arbitrary"),
        name="gdn_mixer",
    )(proj, proj, proj, proj, proj, a_t, conv_w, conv_w, conv_w, lanes(a_log), lanes(dt_bias),
      bcast(a_log), bcast(dt_bias), out_norm.reshape(1, LANES))


def _mlstm_kernel(q_ref, k_ref, v_ref, og_ref, if_ref, it_ref, ft_ref, ilane_ref, flane_ref, ib_ref, fb_ref,
                  onorm_ref, o_ref, li_sc, bc_sc, lirow_sc, bcrow_sc, *, seq):
    c = ML_CHUNK
    nchunk = seq // c
    pair = pl.program_id(1)
    heads = range(2)
    incl = _tri(c, lower=True, strict=False)
    tril_f = incl.astype(F32)
    triu_f = _tri(c, lower=False, strict=False).astype(F32)
    lane = lax.broadcasted_iota(jnp.int32, (1, LANES), 1)
    onorm = onorm_ref[...]
    ones_v = jnp.ones((c, ML_DV), F32)

    gates = if_ref[...]
    li_sc[...] = gates + ilane_ref[...]
    bc_sc[...] = -_softplus(-(gates + flane_ref[...]))

    def cum_body(n, _):
        t0 = pl.multiple_of(n * c, c)
        bc_sc[pl.ds(t0, c), :] = _xdot(tril_f, bc_sc[pl.ds(t0, c), :])
        return 0

    lax.fori_loop(0, nchunk, cum_body, 0)
    for j in heads:
        head = 2 * pair + j
        lirow_sc[j] = it_ref[0, j] + ib_ref[pl.ds(head, 1), :]
        bcrow_sc[j] = _xdot(-_softplus(-(ft_ref[0, j] + fb_ref[pl.ds(head, 1), :])), triu_f)

    def take_lane(x, idx):
        return jnp.sum(jnp.where(lane == idx, x, 0.0), axis=-1, keepdims=True)

    def body(n, carry):
        t0 = pl.multiple_of(n * c, c)
        cst = [carry[2 * j] for j in heads]
        mst = [carry[2 * j + 1] for j in heads]
        qp = q_ref[pl.ds(t0, c), :]
        kp = k_ref[pl.ds(t0, c), :] * (ML_DQK ** -0.5)
        hm = [(lane >= j * ML_DQK) & (lane < (j + 1) * ML_DQK) for j in heads]
        qj = [jnp.where(hm[j], qp, 0.0).astype(BF16) for j in heads]
        kj = [jnp.where(hm[j], kp, 0.0) for j in heads]
        va = [jnp.concatenate([v_ref[pl.ds(t0, c), j * ML_DV:(j + 1) * ML_DV], ones_v], axis=1).astype(BF16)
              for j in heads]
        li = [take_lane(li_sc[pl.ds(t0, c), :], 2 * pair + j) for j in heads]
        bcum = [take_lane(bc_sc[pl.ds(t0, c), :], ML_HEADS + 2 * pair + j) for j in heads]
        dmat = [jnp.where(incl, bcum[j] - bcrow_sc[j, pl.ds(n, 1), :] + lirow_sc[j, pl.ds(n, 1), :], -jnp.inf)
                for j in heads]
        m_intra = [jnp.max(d, axis=-1, keepdims=True) for d in dmat]
        qk = [_bdot_nt(qj[j], kj[j]) for j in heads]
        inter = [_bdot(qj[j], cst[j]) for j in heads]
        sqk = [qk[j] * jnp.exp(dmat[j] - m_intra[j]) for j in heads]
        intra = [_bdot(sqk[j], va[j]) for j in heads]
        bl = [b[c - 1:c, :] for b in bcum]
        wk = [bl[j] - bcum[j] + li[j] for j in heads]
        m_chunk = [jnp.max(w, axis=0, keepdims=True) for w in wk]
        kv = [_bdot_tn(kj[j] * jnp.exp(wk[j] - m_chunk[j]), va[j]) for j in heads]
        new_carry = []
        for j in heads:
            a_inter = bcum[j] + mst[j]
            m_t = jnp.maximum(a_inter, m_intra[j])
            s_inter = jnp.exp(a_inter - m_t)
            s_intra = jnp.exp(m_intra[j] - m_t)
            num = s_inter * inter[j][:, :ML_DV] + s_intra * intra[j][:, :ML_DV]
            den = s_inter * inter[j][:, ML_DV:] + s_intra * intra[j][:, ML_DV:]
            hout = num / jnp.maximum(jnp.abs(den), jnp.exp(-m_t))
            og = og_ref[pl.ds(t0, c), j * ML_DV:(j + 1) * ML_DV]
            o_ref[pl.ds(t0, c), j * ML_DV:(j + 1) * ML_DV] = _rms(hout, onorm) * _sigmoid(og)
            m_new = jnp.maximum(bl[j] + mst[j], m_chunk[j])
            fa = jnp.exp(bl[j] + mst[j] - m_new)
            fc = jnp.exp(m_chunk[j] - m_new)
            new_carry += [fa * cst[j] + fc * kv[j], m_new]
        return tuple(new_carry)

    init = (jnp.zeros((LANES, 2 * ML_DV), F32), jnp.zeros((1, 1), F32)) * 2
    lax.fori_loop(0, nchunk, body, init)


def _mlstm_mixer(proj, i_bias, f_bias, out_norm, *, batch, seq, if_col):
    t = proj.shape[0]
    c = ML_CHUNK
    nchunk = seq // c
    hh = ML_HEADS
    npair = hh // 2
    kw = hh * ML_DQK
    vw = hh * ML_DV
    rows = lambda off: proj[:, off:off + hh].reshape(batch, seq, hh).transpose(0, 2, 1).reshape(batch, hh, nchunk, c)
    bcast = lambda v: jnp.broadcast_to(v.astype(F32)[:, None], (hh, LANES))
    lanes = lambda v, off: jnp.pad(v.astype(F32), (off, LANES - hh - off)).reshape(1, LANES)
    fixed = lambda b, p: (0, 0)
    pw = 2 * ML_DV
    return pl.pallas_call(
        functools.partial(_mlstm_kernel, seq=seq),
        grid=(batch, npair),
        in_specs=[pl.BlockSpec((seq, LANES), lambda b, p: (b, p)),
                  pl.BlockSpec((seq, LANES), lambda b, p: (b, kw // LANES + p)),
                  pl.BlockSpec((seq, pw), lambda b, p: (b, 2 * kw // pw + p)),
                  pl.BlockSpec((seq, pw), lambda b, p: (b, (2 * kw + vw) // pw + p)),
                  pl.BlockSpec((seq, LANES), lambda b, p: (b, if_col // LANES)),
                  pl.BlockSpec((1, 2, nchunk, c), lambda b, p: (b, p, 0, 0)),
                  pl.BlockSpec((1, 2, nchunk, c), lambda b, p: (b, p, 0, 0)),
                  pl.BlockSpec((1, LANES), fixed),
                  pl.BlockSpec((1, LANES), fixed),
                  pl.BlockSpec((hh, LANES), fixed),
                  pl.BlockSpec((hh, LANES), fixed),
                  pl.BlockSpec((1, LANES), fixed)],
        out_specs=pl.BlockSpec((seq, pw), lambda b, p: (b, p)),
        out_shape=jax.ShapeDtypeStruct((t, vw), F32),
        scratch_shapes=[pltpu.VMEM((seq, LANES), F32),
                        pltpu.VMEM((seq, LANES), F32),
                        pltpu.VMEM((2, nchunk, c), F32),
                        pltpu.VMEM((2, nchunk, c), F32)],
        compiler_params=_params("arbitrary", "arbitrary"),
        name="mlstm_mixer",
    )(proj, proj, proj, proj, proj, rows(if_col), rows(if_col + hh), lanes(i_bias, 0), lanes(f_bias, hh),
      bcast(i_bias), bcast(f_bias), out_norm.reshape(1, LANES))


def _sb_kernel(q_ref, k_ref, v_ref, o_ref, qm_sc, k_sc, vm_sc, mcat_sc, *, seq):
    half = SB_BLOCK
    tile = 2 * half
    ntile = seq // tile
    lane = lax.broadcasted_iota(jnp.int32, (1, LANES), 1)
    head0 = lane < SB_DH
    q = q_ref[...] * (SB_DH ** -0.5)
    qm_sc[0] = jnp.where(head0, q, 0.0).astype(BF16)
    qm_sc[1] = jnp.where(head0, 0.0, q).astype(BF16)
    k_sc[...] = k_ref[...].astype(BF16)
    v = v_ref[...]
    vm_sc[0] = jnp.where(head0, v, 0.0).astype(BF16)
    vm_sc[1] = jnp.where(head0, 0.0, v).astype(BF16)
    r = lax.broadcasted_iota(jnp.int32, (half, half), 0)
    c = lax.broadcasted_iota(jnp.int32, (half, half), 1)
    causal = c < r
    m_half = jnp.concatenate([(r > c).astype(BF16), jnp.ones((half, half), BF16)], axis=1)
    mcat_sc[...] = jnp.concatenate([m_half, m_half], axis=0)

    def suffix_sums(lg):
        hi = lg.astype(BF16)
        lo = (lg - hi.astype(F32)).astype(BF16)
        return jnp.dot(jnp.concatenate([hi, lo], axis=1), mcat_sc[...], preferred_element_type=F32)

    def units(qms, kblks, carries, modes):
        z = [lax.dot_general(qm, kb, _NT, preferred_element_type=F32) for qm, kb in zip(qms, kblks)]
        lg = [-_softplus(zz) for zz in z]
        lg_l = [l[:, :half] for l in lg]
        lg_r = [None if m == "diag_first" else l[:, half:] for l, m in zip(lg, modes)]
        lg_l = [jnp.where(causal, l, 0.0) if m == "diag_first" else l for l, m in zip(lg_l, modes)]
        lg_r = [jnp.where(causal, l, 0.0) if m == "diag_second" else l for l, m in zip(lg_r, modes)]
        sums_r = [None if l is None else suffix_sums(l) for l in lg_r]
        sums_l = [suffix_sums(l) for l in lg_l]
        atts, new_carries = [], []
        for zz, ll, lr, sl_, sr, cy, m in zip(z, lg_l, lg_r, sums_l, sums_r, carries, modes):
            if m == "diag_first":
                att = jnp.where(causal, jnp.exp(zz + ll + sl_[:, :half]), 0.0).astype(BF16)
                atts.append(att)
                new_carries.append(sl_[:, half:])
                continue
            carry_l = sr[:, half:] if cy is None else cy + sr[:, half:]
            tail_r = sr[:, :half] if cy is None else cy + sr[:, :half]
            att_r = jnp.exp(zz[:, half:] + lr + tail_r)
            att_l = jnp.exp(zz[:, :half] + ll + carry_l + sl_[:, :half])
            if m == "diag_second":
                att_r = jnp.where(causal, att_r, 0.0)
            atts.append(jnp.concatenate([att_l.astype(BF16), att_r.astype(BF16)], axis=1))
            new_carries.append(carry_l + sl_[:, half:])
        return atts, new_carries

    def vcat(s0, n):
        return jnp.concatenate([vm_sc[0, pl.ds(s0, n), :], vm_sc[1, pl.ds(s0, n), :]], axis=0)

    def q_tile(qb, _):
        t0 = pl.multiple_of(qb * tile, tile)
        qms = [qm_sc[j, pl.ds(t0 + rr * half, half), :] for rr in range(2) for j in range(2)]
        k_d = k_sc[pl.ds(t0, tile), :]
        atts, carries = units(qms, [k_d[:half], k_d[:half], k_d, k_d], [None] * 4,
                              ["diag_first", "diag_first", "diag_second", "diag_second"])
        acc0 = jnp.dot(jnp.concatenate(atts[:2], axis=1), vcat(t0, half), preferred_element_type=F32)
        acc1 = jnp.dot(jnp.concatenate(atts[2:], axis=1), vcat(t0, tile), preferred_element_type=F32)

        def inner(i, st):
            carries, accs = st
            s0 = pl.multiple_of((qb - 1 - i) * tile, tile)
            kblk = k_sc[pl.ds(s0, tile), :]
            vv = vcat(s0, tile)
            atts, new_c = units(qms, [kblk] * 4, list(carries), ["full"] * 4)
            new_a = [accs[rr] + jnp.dot(jnp.concatenate(atts[2 * rr:2 * rr + 2], axis=1), vv,
                                        preferred_element_type=F32) for rr in range(2)]
            return tuple(new_c), tuple(new_a)

        _, accs = lax.fori_loop(0, qb, inner, (tuple(carries), (acc0, acc1)))
        o_ref[pl.ds(t0, half), :] = accs[0]
        o_ref[pl.ds(t0 + half, half), :] = accs[1]
        return 0

    lax.fori_loop(0, ntile, q_tile, 0)


def _sb_mixer(proj, *, batch, seq):
    t = proj.shape[0]
    npair = SB_HEADS // 2
    w = SB_HEADS * SB_DH
    return pl.pallas_call(
        functools.partial(_sb_kernel, seq=seq),
        grid=(batch, npair),
        in_specs=[pl.BlockSpec((seq, LANES), lambda b, p: (b, p)),
                  pl.BlockSpec((seq, LANES), lambda b, p: (b, w // LANES + p)),
                  pl.BlockSpec((seq, LANES), lambda b, p: (b, 2 * w // LANES + p))],
        out_specs=pl.BlockSpec((seq, LANES), lambda b, p: (b, p)),
        out_shape=jax.ShapeDtypeStruct((t, w), F32),
        scratch_shapes=[pltpu.VMEM((2, seq, LANES), BF16),
                        pltpu.VMEM((seq, LANES), BF16),
                        pltpu.VMEM((2, seq, LANES), BF16),
                        pltpu.VMEM((2 * SB_BLOCK, 2 * SB_BLOCK), BF16)],
        compiler_params=_params("arbitrary", "arbitrary"),
        name="sb_mixer",
    )(proj, proj, proj)


def _pad_cols(w, width):
    return jnp.pad(w, ((0, 0), (0, width - w.shape[1])))


def kernel(x, mem, mem_norm, norm_pre_mix, norm_post_mix, norm_pre_ffn, norm_post_ffn, w_mem_kv, w_out,
           w_ffn_in, w_ffn_out, gdn_w_in, gdn_conv, gdn_a_log, gdn_dt_bias, gdn_out_norm, ml_w_in, ml_i_bias,
           ml_f_bias, ml_out_norm, sb_w_in):
    batch, seq, d = x.shape
    n_mem = mem.shape[1]
    depth = w_out.shape[0]
    d_ff = w_ffn_out.shape[1]
    h = x.reshape(batch * seq, d)
    mem2d = mem.reshape(batch * n_mem, d)
    tm = 512

    gdn_main = 3 * GDN_HEADS * GDN_DK + GDN_HEADS * LANES
    ml_main = 2 * ML_HEADS * ML_DQK + 2 * ML_HEADS * ML_DV
    sb_main = 3 * SB_HEADS * SB_DH

    for layer in range(depth):
        kind, j = layer % 3, layer // 3
        if kind == 0:
            w = gdn_w_in[j]
            gates = w[:, gdn_main:gdn_main + 2 * GDN_HEADS]
            w = jnp.concatenate([w[:, :gdn_main], w[:, gdn_main + 2 * GDN_HEADS:], _pad_cols(gates, LANES)], axis=1)
            main = gdn_main
        elif kind == 1:
            w = ml_w_in[j]
            gates = w[:, ml_main:ml_main + 2 * ML_HEADS]
            w = jnp.concatenate([w[:, :ml_main], w[:, ml_main + 2 * ML_HEADS:], _pad_cols(gates, LANES)], axis=1)
            main = ml_main
        else:
            w = sb_w_in[j]
            main = sb_main
        proj = _norm_matmul(h, norm_pre_mix[layer], w.astype(BF16), tm=tm)
        if kind == 0:
            ymix = _gdn_mixer(proj, gdn_conv[j], gdn_a_log[j], gdn_dt_bias[j], gdn_out_norm[j],
                              batch=batch, seq=seq, ab_col=main + MEM_W)
        elif kind == 1:
            ymix = _mlstm_mixer(proj, ml_i_bias[j], ml_f_bias[j], ml_out_norm[j],
                                batch=batch, seq=seq, if_col=main + MEM_W)
        else:
            ymix = _sb_mixer(proj, batch=batch, seq=seq)
        kv = _norm_matmul(mem2d, mem_norm, w_mem_kv[layer].astype(BF16), tm=n_mem)
        ymem = _mem_attention(proj, kv, batch=batch, seq=seq, n_mem=n_mem, qcol=main, tq=512)
        h = _layer_tail(h, ymix, ymem, w_out[layer].astype(BF16), norm_post_mix[layer], norm_pre_ffn[layer],
                        w_ffn_in[layer].astype(BF16), w_ffn_out[layer].astype(BF16), norm_post_ffn[layer],
                        tm=tm, ff_chunk=d_ff // 2)
    return h.reshape(batch, seq, d)
```

```python
import functools

import jax
import jax.numpy as jnp
from jax import lax
from jax.experimental import pallas as pl
from jax.experimental.pallas import tpu as pltpu

F32 = jnp.float32
BF16 = jnp.bfloat16
EPS = 1e-6
HI = lax.Precision.HIGHEST

LANES = 128
VMEM_LIMIT_BYTES = 56 * 1024 * 1024

GDN_CHUNK = 128
ML_CHUNK = 128
GDN_HEADS = 8
GDN_DK = 128
ML_HEADS = 8
ML_DQK = 64
ML_DV = 128
SB_HEADS = 16
SB_DH = 64
SB_BLOCK = 128
MEM_HEADS = 4
MEM_DH = 128
MEM_W = MEM_HEADS * MEM_DH

_NT = (((1,), (1,)), ((), ()))
_TN = (((0,), (0,)), ((), ()))


def _params(*sem):
    return pltpu.CompilerParams(dimension_semantics=sem, vmem_limit_bytes=VMEM_LIMIT_BYTES)


def _rms(x, g):
    return x * lax.rsqrt(jnp.mean(x * x, axis=-1, keepdims=True) + EPS) * g


def _softplus(x):
    return jnp.maximum(x, 0.0) + jnp.log(1.0 + jnp.exp(-jnp.abs(x)))


def _sigmoid(x):
    return 1.0 / (1.0 + jnp.exp(-x))


def _bdot(a, b):
    return jnp.dot(a.astype(BF16), b.astype(BF16), preferred_element_type=F32)


def _bdot_nt(a, b):
    return lax.dot_general(a.astype(BF16), b.astype(BF16), _NT, preferred_element_type=F32)


def _bdot_tn(a, b):
    return lax.dot_general(a.astype(BF16), b.astype(BF16), _TN, preferred_element_type=F32)


def _xdot(a, b):
    return jnp.dot(a, b, precision=HI, preferred_element_type=F32)


def _norm_matmul_kernel(x_ref, g_ref, w_ref, o_ref, *, col_chunk):
    xn = _rms(x_ref[...], g_ref[...]).astype(BF16)
    n = o_ref.shape[1]
    for c0 in range(0, n, col_chunk):
        c1 = min(c0 + col_chunk, n)
        o_ref[:, c0:c1] = jnp.dot(xn, w_ref[:, c0:c1], preferred_element_type=F32)


def _norm_matmul(x2d, g, w, *, tm, col_chunk=512):
    t, d = x2d.shape
    n = w.shape[1]
    return pl.pallas_call(
        functools.partial(_norm_matmul_kernel, col_chunk=col_chunk),
        grid=(t // tm,),
        in_specs=[pl.BlockSpec((tm, d), lambda i: (i, 0)),
                  pl.BlockSpec((1, d), lambda i: (0, 0)),
                  pl.BlockSpec((d, n), lambda i: (0, 0))],
        out_specs=pl.BlockSpec((tm, n), lambda i: (i, 0)),
        out_shape=jax.ShapeDtypeStruct((t, n), F32),
        compiler_params=_params("arbitrary"),
        name="norm_matmul",
    )(x2d, g.reshape(1, d), w)


def _mem_attn_kernel(q_ref, k_ref, v_ref, o_ref):
    scale = MEM_DH ** -0.5
    for h in range(MEM_HEADS):
        sl = slice(h * MEM_DH, (h + 1) * MEM_DH)
        s = _bdot_nt(q_ref[:, sl], k_ref[:, sl]) * scale
        s = s - jnp.max(s, axis=-1, keepdims=True)
        e = jnp.exp(s)
        p = e / jnp.sum(e, axis=-1, keepdims=True)
        o_ref[:, sl] = _bdot(p, v_ref[:, sl])


def _mem_attention(proj, kv, *, batch, seq, n_mem, qcol, tq):
    t = proj.shape[0]
    nq = seq // tq
    qblk = qcol // MEM_W
    return pl.pallas_call(
        _mem_attn_kernel,
        grid=(batch, nq),
        in_specs=[pl.BlockSpec((tq, MEM_W), lambda b, i: (b * nq + i, qblk)),
                  pl.BlockSpec((n_mem, MEM_W), lambda b, i: (b, 0)),
                  pl.BlockSpec((n_mem, MEM_W), lambda b, i: (b, 1))],
        out_specs=pl.BlockSpec((tq, MEM_W), lambda b, i: (b * nq + i, 0)),
        out_shape=jax.ShapeDtypeStruct((t, MEM_W), F32),
        compiler_params=_params("arbitrary", "arbitrary"),
        name="mem_attention",
    )(proj, kv, kv)


def _tail_kernel(h_ref, ymix_ref, ymem_ref, wo1_ref, wo2_ref, gpm_ref, gpf_ref, win_ref, wout_ref,
                 gqf_ref, o_ref, *, d_ff, ff_chunk):
    y = _bdot(ymix_ref[...], wo1_ref[...]) + _bdot(ymem_ref[...], wo2_ref[...])
    h1 = h_ref[...] + _rms(y, gpm_ref[...])
    n = _rms(h1, gpf_ref[...]).astype(BF16)
    acc = jnp.zeros_like(h1)
    for c0 in range(0, d_ff, ff_chunk):
        g = jnp.dot(n, win_ref[:, c0:c0 + ff_chunk], preferred_element_type=F32)
        u = jnp.dot(n, win_ref[:, d_ff + c0:d_ff + c0 + ff_chunk], preferred_element_type=F32)
        a = (g * _sigmoid(g) * u).astype(BF16)
        acc = acc + jnp.dot(a, wout_ref[c0:c0 + ff_chunk, :], preferred_element_type=F32)
    o_ref[...] = h1 + _rms(acc, gqf_ref[...])


def _layer_tail(h2d, ymix, ymem, wo, g_post_mix, g_pre_ffn, w_in, w_out, g_post_ffn, *, tm, ff_chunk):
    t, d = h2d.shape
    mixw = ymix.shape[1]
    d_ff = w_out.shape[0]
    row = lambda i: (i, 0)
    fixed = lambda i: (0, 0)
    return pl.pallas_call(
        functools.partial(_tail_kernel, d_ff=d_ff, ff_chunk=ff_chunk),
        grid=(t // tm,),
        in_specs=[pl.BlockSpec((tm, d), row),
                  pl.BlockSpec((tm, mixw), row),
                  pl.BlockSpec((tm, MEM_W), row),
                  pl.BlockSpec((mixw, d), fixed),
                  pl.BlockSpec((MEM_W, d), fixed),
                  pl.BlockSpec((1, d), fixed),
                  pl.BlockSpec((1, d), fixed),
                  pl.BlockSpec((d, 2 * d_ff), fixed),
                  pl.BlockSpec((d_ff, d), fixed),
                  pl.BlockSpec((1, d), fixed)],
        out_specs=pl.BlockSpec((tm, d), row),
        out_shape=jax.ShapeDtypeStruct((t, d), F32),
        compiler_params=_params("arbitrary"),
        name="layer_tail",
    )(h2d, ymix, ymem, wo[:mixw], wo[mixw:], g_post_mix.reshape(1, d), g_pre_ffn.reshape(1, d),
      w_in, w_out, g_post_ffn.reshape(1, d))


def _select_lane(x, lane):
    r = lax.broadcasted_iota(jnp.int32, (LANES, LANES), 0)
    return _xdot(x, (r == lane).astype(F32))


def _tri(n, lower, strict):
    r = lax.broadcasted_iota(jnp.int32, (n, n), 0)
    c = lax.broadcasted_iota(jnp.int32, (n, n), 1)
    if lower:
        return (r > c) if strict else (r >= c)
    return (r < c) if strict else (r <= c)


def _gdn_kernel(q_ref, k_ref, v_ref, z_ref, ab_ref, at_ref, wq_ref, wk_ref, wv_ref, alane_ref, dlane_ref,
                alog_ref, dtb_ref, onorm_ref, o_ref, gall_sc, beta_sc, grow_sc, u_sc, wq_sc, kg_sc, p_sc, gl_sc,
                state_sc, *, seq):
    c = GDN_CHUNK
    nchunk = seq // c
    ngroup = nchunk // 2
    pair = pl.program_id(1)
    lane = lax.broadcasted_iota(jnp.int32, (1, LANES), 1)
    incl = _tri(c, lower=True, strict=False)
    strict = _tri(c, lower=True, strict=True)
    tril_f = incl.astype(F32)
    rxc = lax.broadcasted_iota(jnp.int32, (c, c), 0) ^ lax.broadcasted_iota(jnp.int32, (c, c), 1)
    onorm = onorm_ref[...]

    ab = ab_ref[...]
    beta_sc[...] = _sigmoid(ab)
    gall_sc[...] = -jnp.exp(alane_ref[...]) * _softplus(ab + dlane_ref[...])

    def cum_body(i, _):
        t0 = [pl.multiple_of((4 * i + d) * c, c) for d in range(4)]
        cum = [_xdot(tril_f, gall_sc[pl.ds(t, c), :]) for t in t0]
        for t, g in zip(t0, cum):
            gall_sc[pl.ds(t, c), :] = g
        return 0

    lax.fori_loop(0, nchunk // 4, cum_body, 0)
    triu_f = _tri(c, lower=False, strict=False).astype(F32)
    for j in range(2):
        head = 2 * pair + j
        neg_a = -jnp.exp(alog_ref[pl.ds(head, 1), :])
        g_row = neg_a * _softplus(at_ref[0, j] + dtb_ref[pl.ds(head, 1), :])
        grow_sc[j] = _xdot(g_row, triu_f)

    def conv_silu(x_ref, w, t0, first, sl):
        cur = x_ref[pl.ds(t0, c), sl]
        p0 = pl.multiple_of(jnp.maximum(t0 - 8, 0), 8)
        prev = x_ref[pl.ds(p0, 8), sl] * jnp.where(first, 0.0, 1.0)
        xe = jnp.concatenate([prev, cur], axis=0)
        y = (w[0:1] * xe[5:5 + c] + w[1:2] * xe[6:6 + c] + w[2:3] * xe[7:7 + c] + w[3:4] * cur)
        return y * _sigmoid(y)

    def l2n(x):
        return x * lax.rsqrt(jnp.sum(x * x, axis=-1, keepdims=True) + EPS)

    def take_lane(x, idx):
        return jnp.sum(jnp.where(lane == idx, x, 0.0), axis=-1, keepdims=True)

    heads = range(2)

    def prep(n, dn, j):
        t0 = pl.multiple_of(n * c, c)
        first = n == 0
        head = 2 * pair + j
        sl = slice(j * LANES, (j + 1) * LANES)
        qc = l2n(conv_silu(q_ref, wq_ref[:, sl], t0, first, sl)) * (GDN_DK ** -0.5)
        kc = l2n(conv_silu(k_ref, wk_ref[:, sl], t0, first, sl))
        vc = conv_silu(v_ref, wv_ref[:, sl], t0, first, sl)
        gcol = take_lane(gall_sc[pl.ds(t0, c), :], head)
        beta = take_lane(beta_sc[pl.ds(t0, c), :], GDN_HEADS + head)
        g_row = grow_sc[j, pl.ds(n, 1), :]
        decay = jnp.where(incl, jnp.exp(jnp.where(incl, gcol - g_row, 0.0)), 0.0)
        return dict(dn=dn, j=j, qc=qc, kc=kc, vc=vc, gcol=gcol, beta=beta, decay=decay)

    def local_work(g, slot):
        ch = []
        for dn in range(2):
            for j in heads:
                ch.append(prep(2 * g + dn, dn, j))
                yield
        kq = [_bdot_nt(jnp.concatenate([x["kc"], x["qc"]], axis=0), x["kc"]) for x in ch]
        a_mat = [jnp.where(strict, x["beta"] * m[:c] * x["decay"], 0.0) for x, m in zip(ch, kq)]
        nm = [-jnp.where(rxc == 1, a, 0.0) for a in a_mat]
        yield
        for lvl in range(1, c.bit_length() - 1):
            mask = lax.shift_right_logical(rxc, lvl) == 1
            a_l = [jnp.where(mask, a, 0.0) for a in a_mat]
            y = [_bdot(m, a) for m, a in zip(nm, a_l)]
            yield
            ay = [a + yy for a, yy in zip(a_l, y)]
            z = [_bdot(a, m) for a, m in zip(ay, nm)]
            yield
            nm = [m - a - zz for m, a, zz in zip(nm, ay, z)]
        eg = [jnp.exp(x["gcol"]) for x in ch]
        rhs = [jnp.concatenate([x["vc"] * x["beta"], x["kc"] * (x["beta"] * e)], axis=1) for x, e in zip(ch, eg)]
        sol = [r + _bdot(m, r) for r, m in zip(rhs, nm)]
        yield
        for x, s, e, m in zip(ch, sol, eg, kq):
            j, dn = x["j"], x["dn"]
            g_last = x["gcol"][c - 1:c, :]
            u_sc[slot, j, dn] = s[:, :LANES]
            wq_sc[slot, j, dn, 0:c, :] = s[:, LANES:].astype(BF16)
            wq_sc[slot, j, dn, c:2 * c, :] = (x["qc"] * e).astype(BF16)
            kg_sc[slot, j, dn] = (x["kc"] * jnp.exp(g_last - x["gcol"])).astype(BF16)
            p_sc[slot, j, dn] = (m[c:] * x["decay"]).astype(BF16)
            gl_sc[slot, j, dn] = jnp.broadcast_to(jnp.exp(g_last), (1, LANES))

    def recurrence(g, slot):
        for dn in range(2):
            t0 = pl.multiple_of(jnp.maximum(2 * g + dn, 0) * c, c)
            ws = [_bdot(wq_sc[slot, j, dn], state_sc[j]) for j in heads]
            yield
            v_new = [u_sc[slot, j, dn] - ws[j][:c] for j in heads]
            upd = [_bdot_tn(kg_sc[slot, j, dn], v_new[j]) for j in heads]
            o = [ws[j][c:] + _bdot(p_sc[slot, j, dn], v_new[j]) for j in heads]
            yield
            for j in heads:
                sl = slice(j * LANES, (j + 1) * LANES)
                z = z_ref[pl.ds(t0, c), sl]
                o_ref[pl.ds(t0, c), sl] = _rms(o[j], onorm) * (z * _sigmoid(z))
                state_sc[j] = state_sc[j] * gl_sc[slot, j, dn] + upd[j]
            yield

    def interleave(main, side, every):
        k, main_done, side_done = 0, False, False
        while not (main_done and side_done):
            if not main_done:
                main_done = next(main, "end") == "end"
            k += 1
            if not side_done and (main_done or k % every == 0):
                side_done = next(side, "end") == "end"

    for ref in (u_sc, wq_sc, kg_sc, p_sc, gl_sc):
        ref[1] = jnp.zeros(ref.shape[1:], ref.dtype)
    state_sc[...] = jnp.zeros_like(state_sc)

    def double_trip(d, _):
        interleave(local_work(2 * d, 0), recurrence(2 * d - 1, 1), every=3)
        interleave(local_work(2 * d + 1, 1), recurrence(2 * d, 0), every=3)
        return 0

    lax.fori_loop(0, ngroup // 2, double_trip, 0)
    for _ in recurrence(ngroup - 1, 1):
        pass


def _gdn_mixer(proj, conv_w, a_log, dt_bias, out_norm, *, batch, seq, ab_col):
    t = proj.shape[0]
    c = GDN_CHUNK
    nchunk = seq // c
    hh = GDN_HEADS
    npair = hh // 2
    pw = 2 * LANES
    a_t = proj[:, ab_col:ab_col + hh].reshape(batch, seq, hh).transpose(0, 2, 1).reshape(batch, hh, nchunk, c)
    bcast = lambda v: jnp.broadcast_to(v.astype(F32)[:, None], (hh, LANES))
    lanes = lambda v: jnp.pad(v.astype(F32), (0, LANES - hh)).reshape(1, LANES)
    col = lambda off: (lambda b, p: (b, off + p))
    wcol = lambda off: (lambda b, p: (0, off + p))
    fixed = lambda b, p: (0, 0)
    return pl.pallas_call(
        functools.partial(_gdn_kernel, seq=seq),
        grid=(batch, npair),
        in_specs=[pl.BlockSpec((seq, pw), col(0)),
                  pl.BlockSpec((seq, pw), col(npair)),
                  pl.BlockSpec((seq, pw), col(2 * npair)),
                  pl.BlockSpec((seq, pw), col(3 * npair)),
                  pl.BlockSpec((seq, LANES), lambda b, p: (b, ab_col // LANES)),
                  pl.BlockSpec((1, 2, nchunk, c), lambda b, p: (b, p, 0, 0)),
                  pl.BlockSpec((4, pw), wcol(0)),
                  pl.BlockSpec((4, pw), wcol(npair)),
                  pl.BlockSpec((4, pw), wcol(2 * npair)),
                  pl.BlockSpec((1, LANES), fixed),
                  pl.BlockSpec((1, LANES), fixed),
                  pl.BlockSpec((hh, LANES), fixed),
                  pl.BlockSpec((hh, LANES), fixed),
                  pl.BlockSpec((1, LANES), fixed)],
        out_specs=pl.BlockSpec((seq, pw), col(0)),
        out_shape=jax.ShapeDtypeStruct((t, hh * LANES), F32),
        scratch_shapes=[pltpu.VMEM((seq, LANES), F32),
                        pltpu.VMEM((seq, LANES), F32),
                        pltpu.VMEM((2, nchunk, c), F32),
                        pltpu.VMEM((2, 2, 2, c, LANES), F32),
                        pltpu.VMEM((2, 2, 2, 2 * c, LANES), BF16),
                        pltpu.VMEM((2, 2, 2, c, LANES), BF16),
                        pltpu.VMEM((2, 2, 2, c, c), BF16),
                        pltpu.VMEM((2, 2, 2, 1, LANES), F32),
                        pltpu.VMEM((2, GDN_DK, LANES), F32)],
        compiler_params=_params("arbitrary", "arbitrary"),
        name="gdn_mixer",
    )(proj, proj, proj, proj, proj, a_t, conv_w, conv_w, conv_w, lanes(a_log), lanes(dt_bias),
      bcast(a_log), bcast(dt_bias), out_norm.reshape(1, LANES))


def _mlstm_kernel(q_ref, k_ref, v_ref, og_ref, if_ref, it_ref, ft_ref, ilane_ref, flane_ref, ib_ref, fb_ref,
                  onorm_ref, o_ref, li_sc, bc_sc, lirow_sc, bcrow_sc, cst_sc, *, seq):
    c = ML_CHUNK
    nchunk = seq // c
    pair = pl.program_id(1)
    heads = range(2)
    incl = _tri(c, lower=True, strict=False)
    tril_f = incl.astype(F32)
    triu_f = _tri(c, lower=False, strict=False).astype(F32)
    lane = lax.broadcasted_iota(jnp.int32, (1, LANES), 1)
    onorm = onorm_ref[...]
    ones_v = jnp.ones((c, ML_DV), F32)

    gates = if_ref[...]
    li_sc[...] = gates + ilane_ref[...]
    bc_sc[...] = -_softplus(-(gates + flane_ref[...]))

    def cum_body(i, _):
        t0 = [pl.multiple_of((4 * i + d) * c, c) for d in range(4)]
        cum = [_xdot(tril_f, bc_sc[pl.ds(t, c), :]) for t in t0]
        for t, b in zip(t0, cum):
            bc_sc[pl.ds(t, c), :] = b
        return 0

    lax.fori_loop(0, nchunk // 4, cum_body, 0)
    for j in heads:
        head = 2 * pair + j
        lirow_sc[j] = it_ref[0, j] + ib_ref[pl.ds(head, 1), :]
        bcrow_sc[j] = _xdot(-_softplus(-(ft_ref[0, j] + fb_ref[pl.ds(head, 1), :])), triu_f)

    def take_lane(x, idx):
        return jnp.sum(jnp.where(lane == idx, x, 0.0), axis=-1, keepdims=True)

    def body(n, carry):
        t0 = pl.multiple_of(n * c, c)
        mst = list(carry)
        qp = q_ref[pl.ds(t0, c), :]
        kp = k_ref[pl.ds(t0, c), :] * (ML_DQK ** -0.5)
        hm = [(lane >= j * ML_DQK) & (lane < (j + 1) * ML_DQK) for j in heads]
        qj = [jnp.where(hm[j], qp, 0.0).astype(BF16) for j in heads]
        kj = [jnp.where(hm[j], kp, 0.0) for j in heads]
        va = [jnp.concatenate([v_ref[pl.ds(t0, c), j * ML_DV:(j + 1) * ML_DV], ones_v], axis=1).astype(BF16)
              for j in heads]
        li = [take_lane(li_sc[pl.ds(t0, c), :], 2 * pair + j) for j in heads]
        bcum = [take_lane(bc_sc[pl.ds(t0, c), :], ML_HEADS + 2 * pair + j) for j in heads]
        dmat = [jnp.where(incl, bcum[j] - bcrow_sc[j, pl.ds(n, 1), :] + lirow_sc[j, pl.ds(n, 1), :], -jnp.inf)
                for j in heads]
        m_intra = [jnp.max(d, axis=-1, keepdims=True) for d in dmat]
        qk = [_bdot_nt(qj[j], kj[j]) for j in heads]
        inter = [_bdot(qj[j], cst_sc[j]) for j in heads]
        sqk = [qk[j] * jnp.exp(dmat[j] - m_intra[j]) for j in heads]
        intra = [_bdot(sqk[j], va[j]) for j in heads]
        bl = [b[c - 1:c, :] for b in bcum]
        wk = [bl[j] - bcum[j] + li[j] for j in heads]
        m_chunk = [jnp.max(w, axis=0, keepdims=True) for w in wk]
        kv = [_bdot_tn(kj[j] * jnp.exp(wk[j] - m_chunk[j]), va[j]) for j in heads]
        new_carry = []
        for j in heads:
            a_inter = bcum[j] + mst[j]
            m_t = jnp.maximum(a_inter, m_intra[j])
            s_inter = jnp.exp(a_inter - m_t)
            s_intra = jnp.exp(m_intra[j] - m_t)
            num = s_inter * inter[j][:, :ML_DV] + s_intra * intra[j][:, :ML_DV]
            den = s_inter * inter[j][:, ML_DV:] + s_intra * intra[j][:, ML_DV:]
            hout = num / jnp.maximum(jnp.abs(den), jnp.exp(-m_t))
            og = og_ref[pl.ds(t0, c), j * ML_DV:(j + 1) * ML_DV]
            o_ref[pl.ds(t0, c), j * ML_DV:(j + 1) * ML_DV] = _rms(hout, onorm) * _sigmoid(og)
            m_new = jnp.maximum(bl[j] + mst[j], m_chunk[j])
            fa = jnp.exp(bl[j] + mst[j] - m_new)
            fc = jnp.exp(m_chunk[j] - m_new)
            cst_sc[j] = fa * cst_sc[j] + fc * kv[j]
            new_carry.append(m_new)
        return tuple(new_carry)

    cst_sc[...] = jnp.zeros_like(cst_sc)
    lax.fori_loop(0, nchunk, body, (jnp.zeros((1, 1), F32),) * 2)


def _mlstm_mixer(proj, i_bias, f_bias, out_norm, *, batch, seq, if_col):
    t = proj.shape[0]
    c = ML_CHUNK
    nchunk = seq // c
    hh = ML_HEADS
    npair = hh // 2
    kw = hh * ML_DQK
    vw = hh * ML_DV
    rows = lambda off: proj[:, off:off + hh].reshape(batch, seq, hh).transpose(0, 2, 1).reshape(batch, hh, nchunk, c)
    bcast = lambda v: jnp.broadcast_to(v.astype(F32)[:, None], (hh, LANES))
    lanes = lambda v, off: jnp.pad(v.astype(F32), (off, LANES - hh - off)).reshape(1, LANES)
    fixed = lambda b, p: (0, 0)
    pw = 2 * ML_DV
    return pl.pallas_call(
        functools.partial(_mlstm_kernel, seq=seq),
        grid=(batch, npair),
        in_specs=[pl.BlockSpec((seq, LANES), lambda b, p: (b, p)),
                  pl.BlockSpec((seq, LANES), lambda b, p: (b, kw // LANES + p)),
                  pl.BlockSpec((seq, pw), lambda b, p: (b, 2 * kw // pw + p)),
                  pl.BlockSpec((seq, pw), lambda b, p: (b, (2 * kw + vw) // pw + p)),
                  pl.BlockSpec((seq, LANES), lambda b, p: (b, if_col // LANES)),
                  pl.BlockSpec((1, 2, nchunk, c), lambda b, p: (b, p, 0, 0)),
                  pl.BlockSpec((1, 2, nchunk, c), lambda b, p: (b, p, 0, 0)),
                  pl.BlockSpec((1, LANES), fixed),
                  pl.BlockSpec((1, LANES), fixed),
                  pl.BlockSpec((hh, LANES), fixed),
                  pl.BlockSpec((hh, LANES), fixed),
                  pl.BlockSpec((1, LANES), fixed)],
        out_specs=pl.BlockSpec((seq, pw), lambda b, p: (b, p)),
        out_shape=jax.ShapeDtypeStruct((t, vw), F32),
        scratch_shapes=[pltpu.VMEM((seq, LANES), F32),
                        pltpu.VMEM((seq, LANES), F32),
                        pltpu.VMEM((2, nchunk, c), F32),
                        pltpu.VMEM((2, nchunk, c), F32),
                        pltpu.VMEM((2, LANES, 2 * ML_DV), F32)],
        compiler_params=_params("arbitrary", "arbitrary"),
        name="mlstm_mixer",
    )(proj, proj, proj, proj, proj, rows(if_col), rows(if_col + hh), lanes(i_bias, 0), lanes(f_bias, hh),
      bcast(i_bias), bcast(f_bias), out_norm.reshape(1, LANES))


def _sb_kernel(q_ref, k_ref, v_ref, o_ref, qm_sc, k_sc, vm_sc, mcat_sc, z_sc, pend_sc, carry_sc, acc_sc, *,
               seq):
    half = SB_BLOCK
    tile = 2 * half
    ntile = seq // tile
    lane = lax.broadcasted_iota(jnp.int32, (1, LANES), 1)
    head0 = lane < SB_DH
    q = q_ref[...] * (SB_DH ** -0.5)
    qm_sc[0] = jnp.where(head0, q, 0.0).astype(BF16)
    qm_sc[1] = jnp.where(head0, 0.0, q).astype(BF16)
    k_sc[...] = k_ref[...].astype(BF16)
    v = v_ref[...]
    vm_sc[0] = jnp.where(head0, v, 0.0).astype(BF16)
    vm_sc[1] = jnp.where(head0, 0.0, v).astype(BF16)
    r = lax.broadcasted_iota(jnp.int32, (half, half), 0)
    c = lax.broadcasted_iota(jnp.int32, (half, half), 1)
    causal = c < r
    m_half = jnp.concatenate([(r > c).astype(BF16), jnp.ones((half, half), BF16)], axis=1)
    mcat_sc[...] = jnp.concatenate([m_half, m_half], axis=0)

    def suffix_sums(lg):
        hi = lg.astype(BF16)
        lo = (lg - hi.astype(F32)).astype(BF16)
        return jnp.dot(jnp.concatenate([hi, lo], axis=1), mcat_sc[...], preferred_element_type=F32)

    def scores(qms, kblks):
        return [lax.dot_general(qm, kb, _NT, preferred_element_type=F32) for qm, kb in zip(qms, kblks)]

    chains = range(4)

    def log_weights(z, modes):
        lg = [-_softplus(zz) for zz in z]
        lg_l = [l[:, :half] for l in lg]
        lg_r = [None if m == "diag_first" else l[:, half:] for l, m in zip(lg, modes)]
        lg_l = [jnp.where(causal, l, 0.0) if m == "diag_first" else l for l, m in zip(lg_l, modes)]
        lg_r = [jnp.where(causal, l, 0.0) if m == "diag_second" else l for l, m in zip(lg_r, modes)]
        sums_r = [None if l is None else suffix_sums(l) for l in lg_r]
        sums_l = [suffix_sums(l) for l in lg_l]
        return lg_l, lg_r, sums_l, sums_r

    def weights(ch, zz, ll, lr, sl_, sr, mode):
        if mode == "diag_first":
            att = jnp.where(causal, jnp.exp(zz + ll + sl_[:, :half]), 0.0).astype(BF16)
            carry_sc[ch] = sl_[:, half:]
            return jnp.concatenate([att, jnp.zeros_like(att)], axis=1)
        if mode == "diag_second":
            carry_l, tail_r = sr[:, half:], sr[:, :half]
        else:
            cy = carry_sc[ch]
            carry_l, tail_r = cy + sr[:, half:], cy + sr[:, :half]
        att_r = jnp.exp(zz[:, half:] + lr + tail_r)
        att_l = jnp.exp(zz[:, :half] + ll + carry_l + sl_[:, :half])
        if mode == "diag_second":
            att_r = jnp.where(causal, att_r, 0.0)
        carry_sc[ch] = carry_l + sl_[:, half:]
        return jnp.concatenate([att_l.astype(BF16), att_r.astype(BF16)], axis=1)

    def vcat(s0, n):
        return jnp.concatenate([vm_sc[0, pl.ds(s0, n), :], vm_sc[1, pl.ds(s0, n), :]], axis=0)

    def key_tile(kb):
        return pl.multiple_of(jnp.maximum(kb, 0) * tile, tile)

    def q_tile(qb, _):
        t0 = pl.multiple_of(qb * tile, tile)
        qms = [qm_sc[j, pl.ds(t0 + rr * half, half), :] for rr in range(2) for j in range(2)]
        k_d = k_sc[pl.ds(t0, tile), :]
        modes = ["diag_first", "diag_first", "diag_second", "diag_second"]
        z = scores(qms, [k_d[:half], k_d[:half], k_d, k_d])
        parts = log_weights(z, modes)
        for ch in chains:
            rr, j = divmod(ch, 2)
            pend_sc[0, rr, :, j * tile:(j + 1) * tile] = weights(ch, z[ch], *[p[ch] for p in parts], modes[ch])
        acc_sc[...] = jnp.zeros_like(acc_sc)
        z_first = scores(qms, [k_sc[pl.ds(key_tile(qb - 1), tile), :]] * 4)
        for ch in chains:
            z_sc[0, ch] = z_first[ch]

        def inner(i, _):
            slot = i & 1
            kb = qb - 1 - i
            vv = vcat(key_tile(kb + 1), tile)
            pv = [jnp.dot(pend_sc[slot, rr], vv, preferred_element_type=F32) for rr in range(2)]
            z = [z_sc[slot, ch] for ch in chains]
            parts = log_weights(z, ["full"] * 4)
            z_next = scores(qms, [k_sc[pl.ds(key_tile(kb - 1), tile), :]] * 4)
            for ch in chains:
                z_sc[1 - slot, ch] = z_next[ch]
            for ch in chains:
                rr, j = divmod(ch, 2)
                pend_sc[1 - slot, rr, :, j * tile:(j + 1) * tile] = weights(
                    ch, z[ch], *[p[ch] for p in parts], "full")
            for rr in range(2):
                acc_sc[rr] += pv[rr]
            return 0

        lax.fori_loop(0, qb, inner, 0)
        vv = vcat(0, tile)
        for rr in range(2):
            o_ref[pl.ds(t0 + rr * half, half), :] = acc_sc[rr] + jnp.dot(
                pend_sc[qb & 1, rr], vv, preferred_element_type=F32)
        return 0

    lax.fori_loop(0, ntile, q_tile, 0)


def _sb_mixer(proj, *, batch, seq):
    t = proj.shape[0]
    npair = SB_HEADS // 2
    w = SB_HEADS * SB_DH
    return pl.pallas_call(
        functools.partial(_sb_kernel, seq=seq),
        grid=(batch, npair),
        in_specs=[pl.BlockSpec((seq, LANES), lambda b, p: (b, p)),
                  pl.BlockSpec((seq, LANES), lambda b, p: (b, w // LANES + p)),
                  pl.BlockSpec((seq, LANES), lambda b, p: (b, 2 * w // LANES + p))],
        out_specs=pl.BlockSpec((seq, LANES), lambda b, p: (b, p)),
        out_shape=jax.ShapeDtypeStruct((t, w), F32),
        scratch_shapes=[pltpu.VMEM((2, seq, LANES), BF16),
                        pltpu.VMEM((seq, LANES), BF16),
                        pltpu.VMEM((2, seq, LANES), BF16),
                        pltpu.VMEM((2 * SB_BLOCK, 2 * SB_BLOCK), BF16),
                        pltpu.VMEM((2, 4, SB_BLOCK, 2 * SB_BLOCK), F32),
                        pltpu.VMEM((2, 2, SB_BLOCK, 4 * SB_BLOCK), BF16),
                        pltpu.VMEM((4, SB_BLOCK, LANES), F32),
                        pltpu.VMEM((2, SB_BLOCK, LANES), F32)],
        compiler_params=_params("arbitrary", "arbitrary"),
        name="sb_mixer",
    )(proj, proj, proj)


def _pad_cols(w, width):
    return jnp.pad(w, ((0, 0), (0, width - w.shape[1])))


def kernel(x, mem, mem_norm, norm_pre_mix, norm_post_mix, norm_pre_ffn, norm_post_ffn, w_mem_kv, w_out,
           w_ffn_in, w_ffn_out, gdn_w_in, gdn_conv, gdn_a_log, gdn_dt_bias, gdn_out_norm, ml_w_in, ml_i_bias,
           ml_f_bias, ml_out_norm, sb_w_in):
    batch, seq, d = x.shape
    n_mem = mem.shape[1]
    depth = w_out.shape[0]
    d_ff = w_ffn_out.shape[1]
    h = x.reshape(batch * seq, d)
    mem2d = mem.reshape(batch * n_mem, d)
    tm = 512

    gdn_main = 3 * GDN_HEADS * GDN_DK + GDN_HEADS * LANES
    ml_main = 2 * ML_HEADS * ML_DQK + 2 * ML_HEADS * ML_DV
    sb_main = 3 * SB_HEADS * SB_DH

    for layer in range(depth):
        kind, j = layer % 3, layer // 3
        if kind == 0:
            w = gdn_w_in[j]
            gates = w[:, gdn_main:gdn_main + 2 * GDN_HEADS]
            w = jnp.concatenate([w[:, :gdn_main], w[:, gdn_main + 2 * GDN_HEADS:], _pad_cols(gates, LANES)], axis=1)
            main = gdn_main
        elif kind == 1:
            w = ml_w_in[j]
            gates = w[:, ml_main:ml_main + 2 * ML_HEADS]
            w = jnp.concatenate([w[:, :ml_main], w[:, ml_main + 2 * ML_HEADS:], _pad_cols(gates, LANES)], axis=1)
            main = ml_main
        else:
            w = sb_w_in[j]
            main = sb_main
        proj = _norm_matmul(h, norm_pre_mix[layer], w.astype(BF16), tm=tm)
        if kind == 0:
            ymix = _gdn_mixer(proj, gdn_conv[j], gdn_a_log[j], gdn_dt_bias[j], gdn_out_norm[j],
                              batch=batch, seq=seq, ab_col=main + MEM_W)
        elif kind == 1:
            ymix = _mlstm_mixer(proj, ml_i_bias[j], ml_f_bias[j], ml_out_norm[j],
                                batch=batch, seq=seq, if_col=main + MEM_W)
        else:
            ymix = _sb_mixer(proj, batch=batch, seq=seq)
        kv = _norm_matmul(mem2d, mem_norm, w_mem_kv[layer].astype(BF16), tm=n_mem)
        ymem = _mem_attention(proj, kv, batch=batch, seq=seq, n_mem=n_mem, qcol=main, tq=512)
        h = _layer_tail(h, ymix, ymem, w_out[layer].astype(BF16), norm_post_mix[layer], norm_pre_ffn[layer],
                        w_ffn_in[layer].astype(BF16), w_ffn_out[layer].astype(BF16), norm_post_ffn[layer],
                        tm=tm, ff_chunk=d_ff // 2)
    return h.reshape(batch, seq, d)
```

```python
import functools

import jax
import jax.numpy as jnp
from jax import lax
from jax.experimental import pallas as pl
from jax.experimental.pallas import tpu as pltpu

F32 = jnp.float32
BF16 = jnp.bfloat16
EPS = 1e-6
LOG2E = 1.4426950408889634
HI = lax.Precision.HIGHEST

LANES = 128
VMEM_LIMIT_BYTES = 56 * 1024 * 1024

GDN_CHUNK = 128
GDN_GROUP = 4
ML_CHUNK = 128
GDN_HEADS = 8
GDN_DK = 128
ML_HEADS = 8
ML_DQK = 64
ML_DV = 128
SB_HEADS = 16
SB_DH = 64
SB_BLOCK = 128
MEM_HEADS = 4
MEM_DH = 128
MEM_W = MEM_HEADS * MEM_DH

_NT = (((1,), (1,)), ((), ()))
_TN = (((0,), (0,)), ((), ()))


def _params(*sem):
    return pltpu.CompilerParams(dimension_semantics=sem, vmem_limit_bytes=VMEM_LIMIT_BYTES)


def _rms(x, g):
    return x * lax.rsqrt(jnp.mean(x * x, axis=-1, keepdims=True) + EPS) * g


def _softplus(x):
    return jnp.maximum(x, 0.0) + jnp.log(1.0 + jnp.exp(-jnp.abs(x)))


def _sigmoid(x):
    return 1.0 / (1.0 + jnp.exp2(x * -LOG2E))


def _bdot(a, b):
    return jnp.dot(a.astype(BF16), b.astype(BF16), preferred_element_type=F32)


def _bdot_nt(a, b):
    return lax.dot_general(a.astype(BF16), b.astype(BF16), _NT, preferred_element_type=F32)


def _bdot_tn(a, b):
    return lax.dot_general(a.astype(BF16), b.astype(BF16), _TN, preferred_element_type=F32)


def _xdot(a, b):
    return jnp.dot(a, b, precision=HI, preferred_element_type=F32)


N_GATE_ROWS = 16


def _norm_matmul_kernel(*refs, col_chunk, conv_cols, gate_col, blocks_per_seq):
    x_ref, g_ref, w_ref = refs[:3]
    refs = list(refs[3:])
    cw_ref = refs.pop(0) if conv_cols else None
    o_ref = refs.pop(0)
    gt_ref = refs.pop(0) if gate_col is not None else None
    halo_sc = refs.pop(0) if conv_cols else None
    tm, n = o_ref.shape
    xn = _rms(x_ref[...], g_ref[...]).astype(BF16)
    if conv_cols:
        @pl.when(lax.rem(pl.program_id(0), blocks_per_seq) == 0)
        def _():
            halo_sc[...] = jnp.zeros_like(halo_sc)

    bounds = [(c0, min(c0 + col_chunk, n)) for c0 in range(0, n, col_chunk)]
    conv_chunks = [b for b in bounds if b[1] <= conv_cols]
    plain_chunks = [b for b in bounds if b[1] > conv_cols]
    order = []
    while conv_chunks or plain_chunks:
        order += conv_chunks[:1] + plain_chunks[:1]
        conv_chunks, plain_chunks = conv_chunks[1:], plain_chunks[1:]
    for c0, c1 in order:
        raw = jnp.dot(xn, w_ref[:, c0:c1], preferred_element_type=F32)
        if c1 <= conv_cols:
            width = c1 - c0
            xe = jnp.concatenate([halo_sc[:, c0:c1], raw], axis=0).reshape(tm // 8 + 1, 8, width)
            sub = lax.broadcasted_iota(jnp.int32, (1, 8, width), 1)
            cw = cw_ref[:, c0:c1]
            y = cw[3:4].reshape(1, 1, width) * xe[1:]
            for s in range(1, 4):
                rot = pltpu.roll(xe, s, axis=1)
                y = y + cw[3 - s:4 - s].reshape(1, 1, width) * jnp.where(sub >= s, rot[1:], rot[:-1])
            y = y.reshape(tm, width)
            o_ref[:, c0:c1] = y * _sigmoid(y)
            halo_sc[:, c0:c1] = raw[tm - 8:tm]
        else:
            o_ref[:, c0:c1] = raw
            if c0 == gate_col:
                gt_ref[...] = raw.T[:N_GATE_ROWS]


def _norm_matmul(x2d, g, w, *, tm, conv_w=None, gate_col=None, seq=None, col_chunk=512):
    t, d = x2d.shape
    n = w.shape[1]
    conv_cols = 0 if conv_w is None else conv_w.shape[1]
    assert conv_cols % col_chunk == 0 and (gate_col is None or (gate_col % col_chunk == 0 and n - gate_col == LANES))
    in_specs = [pl.BlockSpec((tm, d), lambda i: (i, 0)),
                pl.BlockSpec((1, d), lambda i: (0, 0)),
                pl.BlockSpec((d, n), lambda i: (0, 0))]
    args = [x2d, g.reshape(1, d), w]
    out_specs = [pl.BlockSpec((tm, n), lambda i: (i, 0))]
    out_shape = [jax.ShapeDtypeStruct((t, n), F32)]
    scratch = []
    if conv_cols:
        in_specs.append(pl.BlockSpec(conv_w.shape, lambda i: (0, 0)))
        args.append(conv_w)
        scratch.append(pltpu.VMEM((8, conv_cols), F32))
    if gate_col is not None:
        out_specs.append(pl.BlockSpec((N_GATE_ROWS, tm), lambda i: (0, i)))
        out_shape.append(jax.ShapeDtypeStruct((N_GATE_ROWS, t), F32))
    out = pl.pallas_call(
        functools.partial(_norm_matmul_kernel, col_chunk=col_chunk, conv_cols=conv_cols, gate_col=gate_col,
                          blocks_per_seq=None if seq is None else seq // tm),
        grid=(t // tm,),
        in_specs=in_specs,
        out_specs=out_specs,
        out_shape=out_shape,
        scratch_shapes=scratch,
        compiler_params=_params("arbitrary"),
        name="norm_matmul",
    )(*args)
    return out if gate_col is not None else out[0]


def _mem_attn_kernel(q_ref, k_ref, v_ref, o_ref):
    scale = MEM_DH ** -0.5
    for h in range(MEM_HEADS):
        sl = slice(h * MEM_DH, (h + 1) * MEM_DH)
        s = _bdot_nt(q_ref[:, sl], k_ref[:, sl]) * scale
        s = s - jnp.max(s, axis=-1, keepdims=True)
        e = jnp.exp(s)
        p = e / jnp.sum(e, axis=-1, keepdims=True)
        o_ref[:, sl] = _bdot(p, v_ref[:, sl])


def _mem_attention(proj, kv, *, batch, seq, n_mem, qcol, tq):
    t = proj.shape[0]
    nq = seq // tq
    qblk = qcol // MEM_W
    return pl.pallas_call(
        _mem_attn_kernel,
        grid=(batch, nq),
        in_specs=[pl.BlockSpec((tq, MEM_W), lambda b, i: (b * nq + i, qblk)),
                  pl.BlockSpec((n_mem, MEM_W), lambda b, i: (b, 0)),
                  pl.BlockSpec((n_mem, MEM_W), lambda b, i: (b, 1))],
        out_specs=pl.BlockSpec((tq, MEM_W), lambda b, i: (b * nq + i, 0)),
        out_shape=jax.ShapeDtypeStruct((t, MEM_W), F32),
        compiler_params=_params("arbitrary", "arbitrary"),
        name="mem_attention",
    )(proj, kv, kv)


def _tail_kernel(h_ref, ymix_ref, ymem_ref, wo1_ref, wo2_ref, gpm_ref, gpf_ref, win_ref, wout_ref,
                 gqf_ref, o_ref, *, d_ff, ff_chunk):
    y = _bdot(ymix_ref[...], wo1_ref[...]) + _bdot(ymem_ref[...], wo2_ref[...])
    h1 = h_ref[...] + _rms(y, gpm_ref[...])
    n = _rms(h1, gpf_ref[...]).astype(BF16)
    acc = jnp.zeros_like(h1)
    for c0 in range(0, d_ff, ff_chunk):
        g = jnp.dot(n, win_ref[:, c0:c0 + ff_chunk], preferred_element_type=F32)
        u = jnp.dot(n, win_ref[:, d_ff + c0:d_ff + c0 + ff_chunk], preferred_element_type=F32)
        a = (g * _sigmoid(g) * u).astype(BF16)
        acc = acc + jnp.dot(a, wout_ref[c0:c0 + ff_chunk, :], preferred_element_type=F32)
    o_ref[...] = h1 + _rms(acc, gqf_ref[...])


def _layer_tail(h2d, ymix, ymem, wo, g_post_mix, g_pre_ffn, w_in, w_out, g_post_ffn, *, tm, ff_chunk):
    t, d = h2d.shape
    mixw = ymix.shape[1]
    d_ff = w_out.shape[0]
    row = lambda i: (i, 0)
    fixed = lambda i: (0, 0)
    return pl.pallas_call(
        functools.partial(_tail_kernel, d_ff=d_ff, ff_chunk=ff_chunk),
        grid=(t // tm,),
        in_specs=[pl.BlockSpec((tm, d), row),
                  pl.BlockSpec((tm, mixw), row),
                  pl.BlockSpec((tm, MEM_W), row),
                  pl.BlockSpec((mixw, d), fixed),
                  pl.BlockSpec((MEM_W, d), fixed),
                  pl.BlockSpec((1, d), fixed),
                  pl.BlockSpec((1, d), fixed),
                  pl.BlockSpec((d, 2 * d_ff), fixed),
                  pl.BlockSpec((d_ff, d), fixed),
                  pl.BlockSpec((1, d), fixed)],
        out_specs=pl.BlockSpec((tm, d), row),
        out_shape=jax.ShapeDtypeStruct((t, d), F32),
        compiler_params=_params("arbitrary"),
        name="layer_tail",
    )(h2d, ymix, ymem, wo[:mixw], wo[mixw:], g_post_mix.reshape(1, d), g_pre_ffn.reshape(1, d),
      w_in, w_out, g_post_ffn.reshape(1, d))


def _select_lane(x, lane):
    r = lax.broadcasted_iota(jnp.int32, (LANES, LANES), 0)
    return _xdot(x, (r == lane).astype(F32))


def _tri(n, lower, strict):
    r = lax.broadcasted_iota(jnp.int32, (n, n), 0)
    c = lax.broadcasted_iota(jnp.int32, (n, n), 1)
    if lower:
        return (r > c) if strict else (r >= c)
    return (r < c) if strict else (r <= c)


def _gdn_kernel(q_ref, k_ref, v_ref, z_ref, ab_ref, at_ref, alane_ref, dlane_ref,
                alog_ref, dtb_ref, onorm_ref, o_ref, gall_sc, beta_sc, grow_sc, u_sc, wq_sc, kg_sc, p_sc, gl_sc,
                state_sc, *, seq):
    c = GDN_CHUNK
    nchunk = seq // c
    gsz = GDN_GROUP
    ngroup = nchunk // gsz
    pair = pl.program_id(1)
    lane = lax.broadcasted_iota(jnp.int32, (1, LANES), 1)
    incl = _tri(c, lower=True, strict=False)
    strict = _tri(c, lower=True, strict=True)
    tril_f = incl.astype(F32)
    rxc = lax.broadcasted_iota(jnp.int32, (c, c), 0) ^ lax.broadcasted_iota(jnp.int32, (c, c), 1)
    onorm = onorm_ref[...]

    ab = ab_ref[...]
    beta_sc[...] = _sigmoid(ab)
    gall_sc[...] = -jnp.exp(alane_ref[...]) * _softplus(ab + dlane_ref[...])

    def cum_body(i, _):
        t0 = [pl.multiple_of((4 * i + d) * c, c) for d in range(4)]
        cum = [_xdot(tril_f, gall_sc[pl.ds(t, c), :]) for t in t0]
        for t, g in zip(t0, cum):
            gall_sc[pl.ds(t, c), :] = g
        return 0

    lax.fori_loop(0, nchunk // 4, cum_body, 0)
    triu_f = _tri(c, lower=False, strict=False).astype(F32)
    for j in range(2):
        head = 2 * pair + j
        neg_a = -jnp.exp(alog_ref[pl.ds(head, 1), :])
        g_row = neg_a * _softplus(at_ref[j, 0] + dtb_ref[pl.ds(head, 1), :])
        grow_sc[j] = _xdot(g_row, triu_f)

    def l2n(x):
        return x * lax.rsqrt(jnp.sum(x * x, axis=-1, keepdims=True) + EPS)

    def take_lane(x, idx):
        return jnp.sum(jnp.where(lane == idx, x, 0.0), axis=-1, keepdims=True)

    heads = range(2)

    def prep(n, dn, j):
        t0 = pl.multiple_of(n * c, c)
        head = 2 * pair + j
        sl = slice(j * LANES, (j + 1) * LANES)
        qc = l2n(q_ref[pl.ds(t0, c), sl]) * (GDN_DK ** -0.5)
        kc = l2n(k_ref[pl.ds(t0, c), sl])
        vc = v_ref[pl.ds(t0, c), sl]
        gcol = take_lane(gall_sc[pl.ds(t0, c), :], head)
        beta = take_lane(beta_sc[pl.ds(t0, c), :], GDN_HEADS + head)
        g_row = grow_sc[j, pl.ds(n, 1), :]
        decay = jnp.where(incl, jnp.exp(jnp.where(incl, gcol - g_row, 0.0)), 0.0)
        return dict(dn=dn, j=j, qc=qc, kc=kc, vc=vc, gcol=gcol, beta=beta, decay=decay)

    def local_work(g, slot):
        ch = []
        for dn in range(gsz):
            for j in heads:
                ch.append(prep(gsz * g + dn, dn, j))
                yield
        kq = [_bdot_nt(jnp.concatenate([x["kc"], x["qc"]], axis=0), x["kc"]) for x in ch]
        a_mat = [jnp.where(strict, x["beta"] * m[:c] * x["decay"], 0.0) for x, m in zip(ch, kq)]
        nm = [-jnp.where(rxc == 1, a, 0.0) for a in a_mat]
        yield
        for lvl in range(1, c.bit_length() - 1):
            mask = lax.shift_right_logical(rxc, lvl) == 1
            a_l = [jnp.where(mask, a, 0.0) for a in a_mat]
            y = [_bdot(m, a) for m, a in zip(nm, a_l)]
            yield
            ay = [a + yy for a, yy in zip(a_l, y)]
            z = [_bdot(a, m) for a, m in zip(ay, nm)]
            yield
            nm = [m - a - zz for m, a, zz in zip(nm, ay, z)]
        eg = [jnp.exp(x["gcol"]) for x in ch]
        rhs = [jnp.concatenate([x["vc"] * x["beta"], x["kc"] * (x["beta"] * e)], axis=1) for x, e in zip(ch, eg)]
        sol = [r + _bdot(m, r) for r, m in zip(rhs, nm)]
        yield
        for x, s, e, m in zip(ch, sol, eg, kq):
            j, dn = x["j"], x["dn"]
            g_last = x["gcol"][c - 1:c, :]
            u_sc[slot, j, dn] = s[:, :LANES]
            wq_sc[slot, j, dn, 0:c, :] = s[:, LANES:].astype(BF16)
            wq_sc[slot, j, dn, c:2 * c, :] = (x["qc"] * e).astype(BF16)
            kg_sc[slot, j, dn] = (x["kc"] * jnp.exp(g_last - x["gcol"])).astype(BF16)
            p_sc[slot, j, dn] = (m[c:] * x["decay"]).astype(BF16)
            gl_sc[slot, j, dn] = jnp.broadcast_to(jnp.exp(g_last), (1, LANES))

    def recurrence(g, slot):
        for dn in range(gsz):
            t0 = pl.multiple_of(jnp.maximum(gsz * g + dn, 0) * c, c)
            ws = [_bdot(wq_sc[slot, j, dn], state_sc[j]) for j in heads]
            yield
            v_new = [u_sc[slot, j, dn] - ws[j][:c] for j in heads]
            upd = [_bdot_tn(kg_sc[slot, j, dn], v_new[j]) for j in heads]
            o = [ws[j][c:] + _bdot(p_sc[slot, j, dn], v_new[j]) for j in heads]
            yield
            for j in heads:
                sl = slice(j * LANES, (j + 1) * LANES)
                z = z_ref[pl.ds(t0, c), sl]
                o_ref[pl.ds(t0, c), sl] = _rms(o[j], onorm) * (z * _sigmoid(z))
                state_sc[j] = state_sc[j] * gl_sc[slot, j, dn] + upd[j]
            yield

    def interleave(main, side, every):
        k, main_done, side_done = 0, False, False
        while not (main_done and side_done):
            if not main_done:
                main_done = next(main, "end") == "end"
            k += 1
            if not side_done and (main_done or k % every == 0):
                side_done = next(side, "end") == "end"

    for ref in (u_sc, wq_sc, kg_sc, p_sc, gl_sc):
        ref[1] = jnp.zeros(ref.shape[1:], ref.dtype)
    state_sc[...] = jnp.zeros_like(state_sc)

    def double_trip(d, _):
        interleave(local_work(2 * d, 0), recurrence(2 * d - 1, 1), every=2)
        interleave(local_work(2 * d + 1, 1), recurrence(2 * d, 0), every=2)
        return 0

    lax.fori_loop(0, ngroup // 2, double_trip, 0)
    for _ in recurrence(ngroup - 1, 1):
        pass


def _gdn_mixer(proj, gates_t, a_log, dt_bias, out_norm, *, batch, seq, ab_col):
    t = proj.shape[0]
    c = GDN_CHUNK
    nchunk = seq // c
    hh = GDN_HEADS
    npair = hh // 2
    pw = 2 * LANES
    gsz = GDN_GROUP
    assert nchunk % (2 * gsz) == 0
    a_t = gates_t.reshape(N_GATE_ROWS, batch, nchunk, c)
    bcast = lambda v: jnp.broadcast_to(v.astype(F32)[:, None], (hh, LANES))
    lanes = lambda v: jnp.pad(v.astype(F32), (0, LANES - hh)).reshape(1, LANES)
    col = lambda off: (lambda b, p: (b, off + p))
    fixed = lambda b, p: (0, 0)
    return pl.pallas_call(
        functools.partial(_gdn_kernel, seq=seq),
        grid=(batch, npair),
        in_specs=[pl.BlockSpec((seq, pw), col(0)),
                  pl.BlockSpec((seq, pw), col(npair)),
                  pl.BlockSpec((seq, pw), col(2 * npair)),
                  pl.BlockSpec((seq, pw), col(3 * npair)),
                  pl.BlockSpec((seq, LANES), lambda b, p: (b, ab_col // LANES)),
                  pl.BlockSpec((2, 1, nchunk, c), lambda b, p: (p, b, 0, 0)),
                  pl.BlockSpec((1, LANES), fixed),
                  pl.BlockSpec((1, LANES), fixed),
                  pl.BlockSpec((hh, LANES), fixed),
                  pl.BlockSpec((hh, LANES), fixed),
                  pl.BlockSpec((1, LANES), fixed)],
        out_specs=pl.BlockSpec((seq, pw), col(0)),
        out_shape=jax.ShapeDtypeStruct((t, hh * LANES), F32),
        scratch_shapes=[pltpu.VMEM((seq, LANES), F32),
                        pltpu.VMEM((seq, LANES), F32),
                        pltpu.VMEM((2, nchunk, c), F32),
                        pltpu.VMEM((2, 2, gsz, c, LANES), F32),
                        pltpu.VMEM((2, 2, gsz, 2 * c, LANES), BF16),
                        pltpu.VMEM((2, 2, gsz, c, LANES), BF16),
                        pltpu.VMEM((2, 2, gsz, c, c), BF16),
                        pltpu.VMEM((2, 2, gsz, 1, LANES), F32),
                        pltpu.VMEM((2, GDN_DK, LANES), F32)],
        compiler_params=_params("arbitrary", "arbitrary"),
        name="gdn_mixer",
    )(proj, proj, proj, proj, proj, a_t, lanes(a_log), lanes(dt_bias),
      bcast(a_log), bcast(dt_bias), out_norm.reshape(1, LANES))


def _mlstm_kernel(q_ref, k_ref, v_ref, og_ref, if_ref, it_ref, ft_ref, ilane_ref, flane_ref, ib_ref, fb_ref,
                  onorm_ref, o_ref, li_sc, bc_sc, lirow_sc, bcrow_sc, cst_sc, *, seq):
    c = ML_CHUNK
    nchunk = seq // c
    pair = pl.program_id(1)
    heads = range(2)
    incl = _tri(c, lower=True, strict=False)
    tril_f = incl.astype(F32)
    triu_f = _tri(c, lower=False, strict=False).astype(F32)
    lane = lax.broadcasted_iota(jnp.int32, (1, LANES), 1)
    onorm = onorm_ref[...]
    ones_v = jnp.ones((c, ML_DV), F32)

    gates = if_ref[...]
    li_sc[...] = gates + ilane_ref[...]
    bc_sc[...] = -_softplus(-(gates + flane_ref[...]))

    def cum_body(i, _):
        t0 = [pl.multiple_of((4 * i + d) * c, c) for d in range(4)]
        cum = [_xdot(tril_f, bc_sc[pl.ds(t, c), :]) for t in t0]
        for t, b in zip(t0, cum):
            bc_sc[pl.ds(t, c), :] = b
        return 0

    lax.fori_loop(0, nchunk // 4, cum_body, 0)
    for j in heads:
        head = 2 * pair + j
        lirow_sc[j] = it_ref[j, 0] + ib_ref[pl.ds(head, 1), :]
        bcrow_sc[j] = _xdot(-_softplus(-(ft_ref[j, 0] + fb_ref[pl.ds(head, 1), :])), triu_f)

    def take_lane(x, idx):
        return jnp.sum(jnp.where(lane == idx, x, 0.0), axis=-1, keepdims=True)

    def body(n, carry):
        t0 = pl.multiple_of(n * c, c)
        mst = list(carry)
        qp = q_ref[pl.ds(t0, c), :]
        kp = k_ref[pl.ds(t0, c), :] * (ML_DQK ** -0.5)
        hm = [(lane >= j * ML_DQK) & (lane < (j + 1) * ML_DQK) for j in heads]
        qj = [jnp.where(hm[j], qp, 0.0).astype(BF16) for j in heads]
        kj = [jnp.where(hm[j], kp, 0.0) for j in heads]
        va = [jnp.concatenate([v_ref[pl.ds(t0, c), j * ML_DV:(j + 1) * ML_DV], ones_v], axis=1).astype(BF16)
              for j in heads]
        li = [take_lane(li_sc[pl.ds(t0, c), :], 2 * pair + j) for j in heads]
        bcum = [take_lane(bc_sc[pl.ds(t0, c), :], ML_HEADS + 2 * pair + j) for j in heads]
        dmat = [jnp.where(incl, bcum[j] - bcrow_sc[j, pl.ds(n, 1), :] + lirow_sc[j, pl.ds(n, 1), :], -jnp.inf)
                for j in heads]
        m_intra = [jnp.max(d, axis=-1, keepdims=True) for d in dmat]
        qk = [_bdot_nt(qj[j], kj[j]) for j in heads]
        inter = [_bdot(qj[j], cst_sc[j]) for j in heads]
        sqk = [qk[j] * jnp.exp(dmat[j] - m_intra[j]) for j in heads]
        intra = [_bdot(sqk[j], va[j]) for j in heads]
        bl = [b[c - 1:c, :] for b in bcum]
        wk = [bl[j] - bcum[j] + li[j] for j in heads]
        m_chunk = [jnp.max(w, axis=0, keepdims=True) for w in wk]
        kv = [_bdot_tn(kj[j] * jnp.exp(wk[j] - m_chunk[j]), va[j]) for j in heads]
        new_carry = []
        for j in heads:
            a_inter = bcum[j] + mst[j]
            m_t = jnp.maximum(a_inter, m_intra[j])
            s_inter = jnp.exp(a_inter - m_t)
            s_intra = jnp.exp(m_intra[j] - m_t)
            num = s_inter * inter[j][:, :ML_DV] + s_intra * intra[j][:, :ML_DV]
            den = s_inter * inter[j][:, ML_DV:] + s_intra * intra[j][:, ML_DV:]
            hout = num / jnp.maximum(jnp.abs(den), jnp.exp(-m_t))
            og = og_ref[pl.ds(t0, c), j * ML_DV:(j + 1) * ML_DV]
            o_ref[pl.ds(t0, c), j * ML_DV:(j + 1) * ML_DV] = _rms(hout, onorm) * _sigmoid(og)
            m_new = jnp.maximum(bl[j] + mst[j], m_chunk[j])
            fa = jnp.exp(bl[j] + mst[j] - m_new)
            fc = jnp.exp(m_chunk[j] - m_new)
            cst_sc[j] = fa * cst_sc[j] + fc * kv[j]
            new_carry.append(m_new)
        return tuple(new_carry)

    cst_sc[...] = jnp.zeros_like(cst_sc)
    lax.fori_loop(0, nchunk, body, (jnp.zeros((1, 1), F32),) * 2)


def _mlstm_mixer(proj, gates_t, i_bias, f_bias, out_norm, *, batch, seq, if_col):
    t = proj.shape[0]
    c = ML_CHUNK
    nchunk = seq // c
    hh = ML_HEADS
    npair = hh // 2
    kw = hh * ML_DQK
    vw = hh * ML_DV
    g_t = gates_t.reshape(N_GATE_ROWS, batch, nchunk, c)
    bcast = lambda v: jnp.broadcast_to(v.astype(F32)[:, None], (hh, LANES))
    lanes = lambda v, off: jnp.pad(v.astype(F32), (off, LANES - hh - off)).reshape(1, LANES)
    fixed = lambda b, p: (0, 0)
    pw = 2 * ML_DV
    return pl.pallas_call(
        functools.partial(_mlstm_kernel, seq=seq),
        grid=(batch, npair),
        in_specs=[pl.BlockSpec((seq, LANES), lambda b, p: (b, p)),
                  pl.BlockSpec((seq, LANES), lambda b, p: (b, kw // LANES + p)),
                  pl.BlockSpec((seq, pw), lambda b, p: (b, 2 * kw // pw + p)),
                  pl.BlockSpec((seq, pw), lambda b, p: (b, (2 * kw + vw) // pw + p)),
                  pl.BlockSpec((seq, LANES), lambda b, p: (b, if_col // LANES)),
                  pl.BlockSpec((2, 1, nchunk, c), lambda b, p: (p, b, 0, 0)),
                  pl.BlockSpec((2, 1, nchunk, c), lambda b, p: (npair + p, b, 0, 0)),
                  pl.BlockSpec((1, LANES), fixed),
                  pl.BlockSpec((1, LANES), fixed),
                  pl.BlockSpec((hh, LANES), fixed),
                  pl.BlockSpec((hh, LANES), fixed),
                  pl.BlockSpec((1, LANES), fixed)],
        out_specs=pl.BlockSpec((seq, pw), lambda b, p: (b, p)),
        out_shape=jax.ShapeDtypeStruct((t, vw), F32),
        scratch_shapes=[pltpu.VMEM((seq, LANES), F32),
                        pltpu.VMEM((seq, LANES), F32),
                        pltpu.VMEM((2, nchunk, c), F32),
                        pltpu.VMEM((2, nchunk, c), F32),
                        pltpu.VMEM((2, LANES, 2 * ML_DV), F32)],
        compiler_params=_params("arbitrary", "arbitrary"),
        name="mlstm_mixer",
    )(proj, proj, proj, proj, proj, g_t, g_t, lanes(i_bias, 0), lanes(f_bias, hh),
      bcast(i_bias), bcast(f_bias), out_norm.reshape(1, LANES))


def _sb_kernel(q_ref, k_ref, v_ref, o_ref, qm_sc, k_sc, vm_sc, mcat_sc, z_sc, pend_sc, carry_sc, acc_sc, *,
               seq):
    half = SB_BLOCK
    tile = 2 * half
    ntile = seq // tile
    lane = lax.broadcasted_iota(jnp.int32, (1, LANES), 1)
    head0 = lane < SB_DH
    q = q_ref[...] * (SB_DH ** -0.5)
    qm_sc[0] = jnp.where(head0, q, 0.0).astype(BF16)
    qm_sc[1] = jnp.where(head0, 0.0, q).astype(BF16)
    k_sc[...] = k_ref[...].astype(BF16)
    v = v_ref[...]
    vm_sc[0] = jnp.where(head0, v, 0.0).astype(BF16)
    vm_sc[1] = jnp.where(head0, 0.0, v).astype(BF16)
    r = lax.broadcasted_iota(jnp.int32, (half, half), 0)
    c = lax.broadcasted_iota(jnp.int32, (half, half), 1)
    causal = c < r
    m_half = jnp.concatenate([(r > c).astype(BF16), jnp.ones((half, half), BF16)], axis=1)
    mcat_sc[...] = jnp.concatenate([m_half, m_half], axis=0)

    def suffix_sums(lg):
        hi = lg.astype(BF16)
        lo = (lg - hi.astype(F32)).astype(BF16)
        return jnp.dot(jnp.concatenate([hi, lo], axis=1), mcat_sc[...], preferred_element_type=F32)

    def scores(qms, kblks):
        return [lax.dot_general(qm, kb, _NT, preferred_element_type=F32) for qm, kb in zip(qms, kblks)]

    chains = range(4)

    def log_weights(z, modes):
        lg = [-_softplus(zz) for zz in z]
        lg_l = [l[:, :half] for l in lg]
        lg_r = [None if m == "diag_first" else l[:, half:] for l, m in zip(lg, modes)]
        lg_l = [jnp.where(causal, l, 0.0) if m == "diag_first" else l for l, m in zip(lg_l, modes)]
        lg_r = [jnp.where(causal, l, 0.0) if m == "diag_second" else l for l, m in zip(lg_r, modes)]
        sums_r = [None if l is None else suffix_sums(l) for l in lg_r]
        sums_l = [suffix_sums(l) for l in lg_l]
        return lg_l, lg_r, sums_l, sums_r

    def weights(ch, zz, ll, lr, sl_, sr, mode):
        if mode == "diag_first":
            att = jnp.where(causal, jnp.exp(zz + ll + sl_[:, :half]), 0.0).astype(BF16)
            carry_sc[ch] = sl_[:, half:]
            return jnp.concatenate([att, jnp.zeros_like(att)], axis=1)
        if mode == "diag_second":
            carry_l, tail_r = sr[:, half:], sr[:, :half]
        else:
            cy = carry_sc[ch]
            carry_l, tail_r = cy + sr[:, half:], cy + sr[:, :half]
        att_r = jnp.exp(zz[:, half:] + lr + tail_r)
        att_l = jnp.exp(zz[:, :half] + ll + carry_l + sl_[:, :half])
        if mode == "diag_second":
            att_r = jnp.where(causal, att_r, 0.0)
        carry_sc[ch] = carry_l + sl_[:, half:]
        return jnp.concatenate([att_l.astype(BF16), att_r.astype(BF16)], axis=1)

    def vcat(s0, n):
        return jnp.concatenate([vm_sc[0, pl.ds(s0, n), :], vm_sc[1, pl.ds(s0, n), :]], axis=0)

    def key_tile(kb):
        return pl.multiple_of(jnp.maximum(kb, 0) * tile, tile)

    def q_tile(qb, _):
        t0 = pl.multiple_of(qb * tile, tile)
        qms = [qm_sc[j, pl.ds(t0 + rr * half, half), :] for rr in range(2) for j in range(2)]
        k_d = k_sc[pl.ds(t0, tile), :]
        modes = ["diag_first", "diag_first", "diag_second", "diag_second"]
        z = scores(qms, [k_d[:half], k_d[:half], k_d, k_d])
        parts = log_weights(z, modes)
        for ch in chains:
            rr, j = divmod(ch, 2)
            pend_sc[0, rr, :, j * tile:(j + 1) * tile] = weights(ch, z[ch], *[p[ch] for p in parts], modes[ch])
        acc_sc[...] = jnp.zeros_like(acc_sc)
        z_first = scores(qms, [k_sc[pl.ds(key_tile(qb - 1), tile), :]] * 4)
        for ch in chains:
            z_sc[0, ch] = z_first[ch]

        def inner(i, _):
            slot = i & 1
            kb = qb - 1 - i
            vv = vcat(key_tile(kb + 1), tile)
            pv = [jnp.dot(pend_sc[slot, rr], vv, preferred_element_type=F32) for rr in range(2)]
            z = [z_sc[slot, ch] for ch in chains]
            parts = log_weights(z, ["full"] * 4)
            z_next = scores(qms, [k_sc[pl.ds(key_tile(kb - 1), tile), :]] * 4)
            for ch in chains:
                z_sc[1 - slot, ch] = z_next[ch]
            for ch in chains:
                rr, j = divmod(ch, 2)
                pend_sc[1 - slot, rr, :, j * tile:(j + 1) * tile] = weights(
                    ch, z[ch], *[p[ch] for p in parts], "full")
            for rr in range(2):
                acc_sc[rr] += pv[rr]
            return 0

        lax.fori_loop(0, qb, inner, 0)
        vv = vcat(0, tile)
        for rr in range(2):
            o_ref[pl.ds(t0 + rr * half, half), :] = acc_sc[rr] + jnp.dot(
                pend_sc[qb & 1, rr], vv, preferred_element_type=F32)
        return 0

    lax.fori_loop(0, ntile, q_tile, 0)


def _sb_mixer(proj, *, batch, seq):
    t = proj.shape[0]
    npair = SB_HEADS // 2
    w = SB_HEADS * SB_DH
    return pl.pallas_call(
        functools.partial(_sb_kernel, seq=seq),
        grid=(batch, npair),
        in_specs=[pl.BlockSpec((seq, LANES), lambda b, p: (b, p)),
                  pl.BlockSpec((seq, LANES), lambda b, p: (b, w // LANES + p)),
                  pl.BlockSpec((seq, LANES), lambda b, p: (b, 2 * w // LANES + p))],
        out_specs=pl.BlockSpec((seq, LANES), lambda b, p: (b, p)),
        out_shape=jax.ShapeDtypeStruct((t, w), F32),
        scratch_shapes=[pltpu.VMEM((2, seq, LANES), BF16),
                        pltpu.VMEM((seq, LANES), BF16),
                        pltpu.VMEM((2, seq, LANES), BF16),
                        pltpu.VMEM((2 * SB_BLOCK, 2 * SB_BLOCK), BF16),
                        pltpu.VMEM((2, 4, SB_BLOCK, 2 * SB_BLOCK), F32),
                        pltpu.VMEM((2, 2, SB_BLOCK, 4 * SB_BLOCK), BF16),
                        pltpu.VMEM((4, SB_BLOCK, LANES), F32),
                        pltpu.VMEM((2, SB_BLOCK, LANES), F32)],
        compiler_params=_params("arbitrary", "arbitrary"),
        name="sb_mixer",
    )(proj, proj, proj)


def _pad_cols(w, width):
    return jnp.pad(w, ((0, 0), (0, width - w.shape[1])))


def kernel(x, mem, mem_norm, norm_pre_mix, norm_post_mix, norm_pre_ffn, norm_post_ffn, w_mem_kv, w_out,
           w_ffn_in, w_ffn_out, gdn_w_in, gdn_conv, gdn_a_log, gdn_dt_bias, gdn_out_norm, ml_w_in, ml_i_bias,
           ml_f_bias, ml_out_norm, sb_w_in):
    batch, seq, d = x.shape
    n_mem = mem.shape[1]
    depth = w_out.shape[0]
    d_ff = w_ffn_out.shape[1]
    h = x.reshape(batch * seq, d)
    mem2d = mem.reshape(batch * n_mem, d)
    tm = 512

    gdn_main = 3 * GDN_HEADS * GDN_DK + GDN_HEADS * LANES
    ml_main = 2 * ML_HEADS * ML_DQK + 2 * ML_HEADS * ML_DV
    sb_main = 3 * SB_HEADS * SB_DH

    for layer in range(depth):
        kind, j = layer % 3, layer // 3
        if kind == 0:
            w = gdn_w_in[j]
            gates = w[:, gdn_main:gdn_main + 2 * GDN_HEADS]
            w = jnp.concatenate([w[:, :gdn_main], w[:, gdn_main + 2 * GDN_HEADS:], _pad_cols(gates, LANES)], axis=1)
            main = gdn_main
        elif kind == 1:
            w = ml_w_in[j]
            gates = w[:, ml_main:ml_main + 2 * ML_HEADS]
            w = jnp.concatenate([w[:, :ml_main], w[:, ml_main + 2 * ML_HEADS:], _pad_cols(gates, LANES)], axis=1)
            main = ml_main
        else:
            w = sb_w_in[j]
            main = sb_main
        if kind == 0:
            proj, gates_t = _norm_matmul(h, norm_pre_mix[layer], w.astype(BF16), tm=tm, conv_w=gdn_conv[j],
                                         gate_col=main + MEM_W, seq=seq)
            ymix = _gdn_mixer(proj, gates_t, gdn_a_log[j], gdn_dt_bias[j], gdn_out_norm[j],
                              batch=batch, seq=seq, ab_col=main + MEM_W)
        elif kind == 1:
            proj, gates_t = _norm_matmul(h, norm_pre_mix[layer], w.astype(BF16), tm=tm, gate_col=main + MEM_W)
            ymix = _mlstm_mixer(proj, gates_t, ml_i_bias[j], ml_f_bias[j], ml_out_norm[j],
                                batch=batch, seq=seq, if_col=main + MEM_W)
        else:
            proj = _norm_matmul(h, norm_pre_mix[layer], w.astype(BF16), tm=tm)
            ymix = _sb_mixer(proj, batch=batch, seq=seq)
        kv = _norm_matmul(mem2d, mem_norm, w_mem_kv[layer].astype(BF16), tm=n_mem)
        ymem = _mem_attention(proj, kv, batch=batch, seq=seq, n_mem=n_mem, qcol=main, tq=512)
        h = _layer_tail(h, ymix, ymem, w_out[layer].astype(BF16), norm_post_mix[layer], norm_pre_ffn[layer],
                        w_ffn_in[layer].astype(BF16), w_ffn_out[layer].astype(BF16), norm_post_ffn[layer],
                        tm=tm, ff_chunk=d_ff // 2)
    return h.reshape(batch, seq, d)
```

```python
import functools

import jax
import jax.numpy as jnp
from jax import lax
from jax.experimental import pallas as pl
from jax.experimental.pallas import tpu as pltpu

F32 = jnp.float32
BF16 = jnp.bfloat16
EPS = 1e-6
LOG2E = 1.4426950408889634
HI = lax.Precision.HIGHEST

LANES = 128
VMEM_LIMIT_BYTES = 56 * 1024 * 1024

GDN_CHUNK = 128
GDN_GROUP = 4
ML_CHUNK = 128
GDN_HEADS = 8
GDN_DK = 128
ML_HEADS = 8
ML_DQK = 64
ML_DV = 128
SB_HEADS = 16
SB_DH = 64
SB_BLOCK = 128
MEM_HEADS = 4
MEM_DH = 128
MEM_W = MEM_HEADS * MEM_DH

_NT = (((1,), (1,)), ((), ()))
_TN = (((0,), (0,)), ((), ()))


def _params(*sem):
    return pltpu.CompilerParams(dimension_semantics=sem, vmem_limit_bytes=VMEM_LIMIT_BYTES)


def _rms(x, g):
    return x * lax.rsqrt(jnp.mean(x * x, axis=-1, keepdims=True) + EPS) * g


def _softplus(x):
    return jnp.maximum(x, 0.0) + jnp.log(1.0 + jnp.exp(-jnp.abs(x)))


def _sigmoid(x):
    return 1.0 / (1.0 + jnp.exp2(x * -LOG2E))


def _bdot(a, b):
    return jnp.dot(a.astype(BF16), b.astype(BF16), preferred_element_type=F32)


def _bdot_nt(a, b):
    return lax.dot_general(a.astype(BF16), b.astype(BF16), _NT, preferred_element_type=F32)


def _bdot_tn(a, b):
    return lax.dot_general(a.astype(BF16), b.astype(BF16), _TN, preferred_element_type=F32)


def _xdot(a, b):
    return jnp.dot(a, b, precision=HI, preferred_element_type=F32)


def _tril_cumsum(x):
    n = x.shape[0]
    tril = _tri(n, lower=True, strict=False).astype(BF16)
    hi = x.astype(BF16)
    r = x - hi.astype(F32)
    mid = r.astype(BF16)
    lo = (r - mid.astype(F32)).astype(BF16)
    return jnp.dot(jnp.concatenate([tril, tril, tril], axis=1), jnp.concatenate([hi, mid, lo], axis=0),
                   preferred_element_type=F32)


N_GATE_ROWS = 16


def _norm_matmul_kernel(*refs, col_chunk, conv_cols, gate_col, blocks_per_seq):
    x_ref, g_ref, w_ref = refs[:3]
    refs = list(refs[3:])
    cw_ref = refs.pop(0) if conv_cols else None
    o_ref = refs.pop(0)
    gt_ref = refs.pop(0) if gate_col is not None else None
    halo_sc = refs.pop(0) if conv_cols else None
    tm, n = o_ref.shape
    xn = _rms(x_ref[...], g_ref[...]).astype(BF16)
    if conv_cols:
        @pl.when(lax.rem(pl.program_id(0), blocks_per_seq) == 0)
        def _():
            halo_sc[...] = jnp.zeros_like(halo_sc)

    bounds = [(c0, min(c0 + col_chunk, n)) for c0 in range(0, n, col_chunk)]
    conv_chunks = [b for b in bounds if b[1] <= conv_cols]
    plain_chunks = [b for b in bounds if b[1] > conv_cols]
    order = []
    while conv_chunks or plain_chunks:
        order += conv_chunks[:1] + plain_chunks[:1]
        conv_chunks, plain_chunks = conv_chunks[1:], plain_chunks[1:]
    for c0, c1 in order:
        raw = jnp.dot(xn, w_ref[:, c0:c1], preferred_element_type=F32)
        if c1 <= conv_cols:
            width = c1 - c0
            xe = jnp.concatenate([halo_sc[:, c0:c1], raw], axis=0).reshape(tm // 8 + 1, 8, width)
            sub = lax.broadcasted_iota(jnp.int32, (1, 8, width), 1)
            cw = cw_ref[:, c0:c1]
            y = cw[3:4].reshape(1, 1, width) * xe[1:]
            for s in range(1, 4):
                rot = pltpu.roll(xe, s, axis=1)
                y = y + cw[3 - s:4 - s].reshape(1, 1, width) * jnp.where(sub >= s, rot[1:], rot[:-1])
            y = y.reshape(tm, width)
            o_ref[:, c0:c1] = y * _sigmoid(y)
            halo_sc[:, c0:c1] = raw[tm - 8:tm]
        else:
            o_ref[:, c0:c1] = raw
            if c0 == gate_col:
                gt_ref[...] = raw.T[:N_GATE_ROWS]


def _norm_matmul(x2d, g, w, *, tm, conv_w=None, gate_col=None, seq=None, col_chunk=512):
    t, d = x2d.shape
    n = w.shape[1]
    conv_cols = 0 if conv_w is None else conv_w.shape[1]
    assert conv_cols % col_chunk == 0 and (gate_col is None or (gate_col % col_chunk == 0 and n - gate_col == LANES))
    in_specs = [pl.BlockSpec((tm, d), lambda i: (i, 0)),
                pl.BlockSpec((1, d), lambda i: (0, 0)),
                pl.BlockSpec((d, n), lambda i: (0, 0))]
    args = [x2d, g.reshape(1, d), w]
    out_specs = [pl.BlockSpec((tm, n), lambda i: (i, 0))]
    out_shape = [jax.ShapeDtypeStruct((t, n), F32)]
    scratch = []
    if conv_cols:
        in_specs.append(pl.BlockSpec(conv_w.shape, lambda i: (0, 0)))
        args.append(conv_w)
        scratch.append(pltpu.VMEM((8, conv_cols), F32))
    if gate_col is not None:
        out_specs.append(pl.BlockSpec((N_GATE_ROWS, tm), lambda i: (0, i)))
        out_shape.append(jax.ShapeDtypeStruct((N_GATE_ROWS, t), F32))
    out = pl.pallas_call(
        functools.partial(_norm_matmul_kernel, col_chunk=col_chunk, conv_cols=conv_cols, gate_col=gate_col,
                          blocks_per_seq=None if seq is None else seq // tm),
        grid=(t // tm,),
        in_specs=in_specs,
        out_specs=out_specs,
        out_shape=out_shape,
        scratch_shapes=scratch,
        compiler_params=_params("arbitrary"),
        name="norm_matmul",
    )(*args)
    return out if gate_col is not None else out[0]


def _mem_attention(q_ref, k_ref, v_ref):
    scale = MEM_DH ** -0.5
    heads = []
    for h in range(MEM_HEADS):
        sl = slice(h * MEM_DH, (h + 1) * MEM_DH)
        s = _bdot_nt(q_ref[:, sl], k_ref[:, sl]) * scale
        e = jnp.exp(s - jnp.max(s, axis=-1, keepdims=True))
        p = e / jnp.sum(e, axis=-1, keepdims=True)
        heads.append(_bdot(p, v_ref[:, sl]).astype(BF16))
    return jnp.concatenate(heads, axis=1)


def _tail_kernel(h_ref, ymix_ref, q_ref, k_ref, v_ref, wo1_ref, wo2_ref, gpm_ref, gpf_ref, win_ref, wout_ref,
                 gqf_ref, o_ref, *, d_ff, ff_chunk):
    ymem = _mem_attention(q_ref, k_ref, v_ref)
    y = _bdot(ymix_ref[...], wo1_ref[...]) + jnp.dot(ymem, wo2_ref[...], preferred_element_type=F32)
    h1 = h_ref[...] + _rms(y, gpm_ref[...])
    n = _rms(h1, gpf_ref[...]).astype(BF16)
    acc = jnp.zeros_like(h1)
    for c0 in range(0, d_ff, ff_chunk):
        g = jnp.dot(n, win_ref[:, c0:c0 + ff_chunk], preferred_element_type=F32)
        u = jnp.dot(n, win_ref[:, d_ff + c0:d_ff + c0 + ff_chunk], preferred_element_type=F32)
        a = (g * _sigmoid(g) * u).astype(BF16)
        acc = acc + jnp.dot(a, wout_ref[c0:c0 + ff_chunk, :], preferred_element_type=F32)
    o_ref[...] = h1 + _rms(acc, gqf_ref[...])


def _layer_tail(h2d, ymix, proj, kv, wo, g_post_mix, g_pre_ffn, w_in, w_out, g_post_ffn, *, tm, ff_chunk, seq,
                n_mem, qcol):
    t, d = h2d.shape
    mixw = ymix.shape[1]
    d_ff = w_out.shape[0]
    blocks_per_seq = seq // tm
    row = lambda i: (i, 0)
    fixed = lambda i: (0, 0)
    return pl.pallas_call(
        functools.partial(_tail_kernel, d_ff=d_ff, ff_chunk=ff_chunk),
        grid=(t // tm,),
        in_specs=[pl.BlockSpec((tm, d), row),
                  pl.BlockSpec((tm, mixw), row),
                  pl.BlockSpec((tm, MEM_W), lambda i: (i, qcol // MEM_W)),
                  pl.BlockSpec((n_mem, MEM_W), lambda i: (i // blocks_per_seq, 0)),
                  pl.BlockSpec((n_mem, MEM_W), lambda i: (i // blocks_per_seq, 1)),
                  pl.BlockSpec((mixw, d), fixed),
                  pl.BlockSpec((MEM_W, d), fixed),
                  pl.BlockSpec((1, d), fixed),
                  pl.BlockSpec((1, d), fixed),
                  pl.BlockSpec((d, 2 * d_ff), fixed),
                  pl.BlockSpec((d_ff, d), fixed),
                  pl.BlockSpec((1, d), fixed)],
        out_specs=pl.BlockSpec((tm, d), row),
        out_shape=jax.ShapeDtypeStruct((t, d), F32),
        compiler_params=_params("arbitrary"),
        name="layer_tail",
    )(h2d, ymix, proj, kv, kv, wo[:mixw], wo[mixw:], g_post_mix.reshape(1, d), g_pre_ffn.reshape(1, d),
      w_in, w_out, g_post_ffn.reshape(1, d))


def _tri(n, lower, strict):
    r = lax.broadcasted_iota(jnp.int32, (n, n), 0)
    c = lax.broadcasted_iota(jnp.int32, (n, n), 1)
    if lower:
        return (r > c) if strict else (r >= c)
    return (r < c) if strict else (r <= c)


def _gdn_kernel(q_ref, k_ref, v_ref, z_ref, ab_ref, at_ref, alane_ref, dlane_ref,
                alog_ref, dtb_ref, onorm_ref, o_ref, gall_sc, beta_sc, grow_sc, u_sc, wq_sc, kg_sc, p_sc, gl_sc,
                state_sc, *, seq):
    c = GDN_CHUNK
    nchunk = seq // c
    gsz = GDN_GROUP
    ngroup = nchunk // gsz
    pair = pl.program_id(1)
    lane = lax.broadcasted_iota(jnp.int32, (1, LANES), 1)
    incl = _tri(c, lower=True, strict=False)
    strict = _tri(c, lower=True, strict=True)
    rxc = lax.broadcasted_iota(jnp.int32, (c, c), 0) ^ lax.broadcasted_iota(jnp.int32, (c, c), 1)
    onorm = onorm_ref[...]

    ab = ab_ref[...]
    beta_sc[...] = _sigmoid(ab)
    gall_sc[...] = -jnp.exp(alane_ref[...]) * _softplus(ab + dlane_ref[...])

    def cum_body(i, _):
        t0 = [pl.multiple_of((4 * i + d) * c, c) for d in range(4)]
        cum = [_tril_cumsum(gall_sc[pl.ds(t, c), :]) for t in t0]
        for t, g in zip(t0, cum):
            gall_sc[pl.ds(t, c), :] = g
        return 0

    lax.fori_loop(0, nchunk // 4, cum_body, 0)
    triu_f = _tri(c, lower=False, strict=False).astype(F32)
    for j in range(2):
        head = 2 * pair + j
        neg_a = -jnp.exp(alog_ref[pl.ds(head, 1), :])
        g_row = neg_a * _softplus(at_ref[j, 0] + dtb_ref[pl.ds(head, 1), :])
        grow_sc[j] = _xdot(g_row, triu_f)

    def l2n(x):
        return x * lax.rsqrt(jnp.sum(x * x, axis=-1, keepdims=True) + EPS)

    def take_lane(x, idx):
        return jnp.sum(jnp.where(lane == idx, x, 0.0), axis=-1, keepdims=True)

    heads = range(2)

    def prep(n, dn, j):
        t0 = pl.multiple_of(n * c, c)
        head = 2 * pair + j
        sl = slice(j * LANES, (j + 1) * LANES)
        qc = l2n(q_ref[pl.ds(t0, c), sl]) * (GDN_DK ** -0.5)
        kc = l2n(k_ref[pl.ds(t0, c), sl])
        vc = v_ref[pl.ds(t0, c), sl]
        gcol = take_lane(gall_sc[pl.ds(t0, c), :], head)
        beta = take_lane(beta_sc[pl.ds(t0, c), :], GDN_HEADS + head)
        g_row = grow_sc[j, pl.ds(n, 1), :]
        decay = jnp.where(incl, jnp.exp(jnp.where(incl, gcol - g_row, 0.0)), 0.0)
        return dict(dn=dn, j=j, qc=qc, kc=kc, vc=vc, gcol=gcol, beta=beta, decay=decay)

    def local_work(g, slot):
        ch = []
        for dn in range(gsz):
            for j in heads:
                ch.append(prep(gsz * g + dn, dn, j))
                yield
        kq = [_bdot_nt(jnp.concatenate([x["kc"], x["qc"]], axis=0), x["kc"]) for x in ch]
        a_mat = [jnp.where(strict, x["beta"] * m[:c] * x["decay"], 0.0) for x, m in zip(ch, kq)]
        nm = [-jnp.where(rxc == 1, a, 0.0) for a in a_mat]
        yield
        for lvl in range(1, c.bit_length() - 1):
            mask = lax.shift_right_logical(rxc, lvl) == 1
            a_l = [jnp.where(mask, a, 0.0) for a in a_mat]
            y = [_bdot(m, a) for m, a in zip(nm, a_l)]
            yield
            ay = [a + yy for a, yy in zip(a_l, y)]
            z = [_bdot(a, m) for a, m in zip(ay, nm)]
            yield
            nm = [m - a - zz for m, a, zz in zip(nm, ay, z)]
        eg = [jnp.exp(x["gcol"]) for x in ch]
        rhs = [jnp.concatenate([x["vc"] * x["beta"], x["kc"] * (x["beta"] * e)], axis=1) for x, e in zip(ch, eg)]
        sol = [r + _bdot(m, r) for r, m in zip(rhs, nm)]
        yield
        for x, s, e, m in zip(ch, sol, eg, kq):
            j, dn = x["j"], x["dn"]
            g_last = x["gcol"][c - 1:c, :]
            u_sc[slot, j, dn] = s[:, :LANES]
            wq_sc[slot, j, dn, 0:c, :] = s[:, LANES:].astype(BF16)
            wq_sc[slot, j, dn, c:2 * c, :] = (x["qc"] * e).astype(BF16)
            kg_sc[slot, j, dn] = (x["kc"] * jnp.exp(g_last - x["gcol"])).astype(BF16)
            p_sc[slot, j, dn] = (m[c:] * x["decay"]).astype(BF16)
            gl_sc[slot, j, dn] = jnp.broadcast_to(jnp.exp(g_last), (1, LANES))

    def recurrence(g, slot):
        for dn in range(gsz):
            t0 = pl.multiple_of(jnp.maximum(gsz * g + dn, 0) * c, c)
            ws = [_bdot(wq_sc[slot, j, dn], state_sc[j]) for j in heads]
            yield
            v_new = [u_sc[slot, j, dn] - ws[j][:c] for j in heads]
            upd = [_bdot_tn(kg_sc[slot, j, dn], v_new[j]) for j in heads]
            o = [ws[j][c:] + _bdot(p_sc[slot, j, dn], v_new[j]) for j in heads]
            yield
            for j in heads:
                sl = slice(j * LANES, (j + 1) * LANES)
                z = z_ref[pl.ds(t0, c), sl]
                o_ref[pl.ds(t0, c), sl] = (_rms(o[j], onorm) * (z * _sigmoid(z))).astype(o_ref.dtype)
                state_sc[j] = state_sc[j] * gl_sc[slot, j, dn] + upd[j]
            yield

    def interleave(main, side, every):
        k, main_done, side_done = 0, False, False
        while not (main_done and side_done):
            if not main_done:
                main_done = next(main, "end") == "end"
            k += 1
            if not side_done and (main_done or k % every == 0):
                side_done = next(side, "end") == "end"

    for ref in (u_sc, wq_sc, kg_sc, p_sc, gl_sc):
        ref[1] = jnp.zeros(ref.shape[1:], ref.dtype)
    state_sc[...] = jnp.zeros_like(state_sc)

    def double_trip(d, _):
        interleave(local_work(2 * d, 0), recurrence(2 * d - 1, 1), every=2)
        interleave(local_work(2 * d + 1, 1), recurrence(2 * d, 0), every=2)
        return 0

    lax.fori_loop(0, ngroup // 2, double_trip, 0)
    for _ in recurrence(ngroup - 1, 1):
        pass


def _gdn_mixer(proj, gates_t, a_log, dt_bias, out_norm, *, batch, seq, ab_col):
    t = proj.shape[0]
    c = GDN_CHUNK
    nchunk = seq // c
    hh = GDN_HEADS
    npair = hh // 2
    pw = 2 * LANES
    gsz = GDN_GROUP
    assert nchunk % (2 * gsz) == 0
    a_t = gates_t.reshape(N_GATE_ROWS, batch, nchunk, c)
    bcast = lambda v: jnp.broadcast_to(v.astype(F32)[:, None], (hh, LANES))
    lanes = lambda v: jnp.pad(v.astype(F32), (0, LANES - hh)).reshape(1, LANES)
    col = lambda off: (lambda b, p: (b, off + p))
    fixed = lambda b, p: (0, 0)
    return pl.pallas_call(
        functools.partial(_gdn_kernel, seq=seq),
        grid=(batch, npair),
        in_specs=[pl.BlockSpec((seq, pw), col(0)),
                  pl.BlockSpec((seq, pw), col(npair)),
                  pl.BlockSpec((seq, pw), col(2 * npair)),
                  pl.BlockSpec((seq, pw), col(3 * npair)),
                  pl.BlockSpec((seq, LANES), lambda b, p: (b, ab_col // LANES)),
                  pl.BlockSpec((2, 1, nchunk, c), lambda b, p: (p, b, 0, 0)),
                  pl.BlockSpec((1, LANES), fixed),
                  pl.BlockSpec((1, LANES), fixed),
                  pl.BlockSpec((hh, LANES), fixed),
                  pl.BlockSpec((hh, LANES), fixed),
                  pl.BlockSpec((1, LANES), fixed)],
        out_specs=pl.BlockSpec((seq, pw), col(0)),
        out_shape=jax.ShapeDtypeStruct((t, hh * LANES), BF16),
        scratch_shapes=[pltpu.VMEM((seq, LANES), F32),
                        pltpu.VMEM((seq, LANES), F32),
                        pltpu.VMEM((2, nchunk, c), F32),
                        pltpu.VMEM((2, 2, gsz, c, LANES), F32),
                        pltpu.VMEM((2, 2, gsz, 2 * c, LANES), BF16),
                        pltpu.VMEM((2, 2, gsz, c, LANES), BF16),
                        pltpu.VMEM((2, 2, gsz, c, c), BF16),
                        pltpu.VMEM((2, 2, gsz, 1, LANES), F32),
                        pltpu.VMEM((2, GDN_DK, LANES), F32)],
        compiler_params=_params("arbitrary", "arbitrary"),
        name="gdn_mixer",
    )(proj, proj, proj, proj, proj, a_t, lanes(a_log), lanes(dt_bias),
      bcast(a_log), bcast(dt_bias), out_norm.reshape(1, LANES))


def _mlstm_kernel(q_ref, k_ref, v_ref, og_ref, if_ref, it_ref, ft_ref, ilane_ref, flane_ref, ib_ref, fb_ref,
                  onorm_ref, o_ref, li_sc, bc_sc, lirow_sc, bcrow_sc, cst_sc, *, seq):
    c = ML_CHUNK
    nchunk = seq // c
    pair = pl.program_id(1)
    heads = range(2)
    incl = _tri(c, lower=True, strict=False)
    triu_f = _tri(c, lower=False, strict=False).astype(F32)
    lane = lax.broadcasted_iota(jnp.int32, (1, LANES), 1)
    onorm = onorm_ref[...]
    ones_v = jnp.ones((c, ML_DV), F32)

    gates = if_ref[...]
    li_sc[...] = gates + ilane_ref[...]
    bc_sc[...] = -_softplus(-(gates + flane_ref[...]))

    def cum_body(i, _):
        t0 = [pl.multiple_of((4 * i + d) * c, c) for d in range(4)]
        cum = [_tril_cumsum(bc_sc[pl.ds(t, c), :]) for t in t0]
        for t, b in zip(t0, cum):
            bc_sc[pl.ds(t, c), :] = b
        return 0

    lax.fori_loop(0, nchunk // 4, cum_body, 0)
    for j in heads:
        head = 2 * pair + j
        lirow_sc[j] = it_ref[j, 0] + ib_ref[pl.ds(head, 1), :]
        bcrow_sc[j] = _xdot(-_softplus(-(ft_ref[j, 0] + fb_ref[pl.ds(head, 1), :])), triu_f)

    def take_lane(x, idx):
        return jnp.sum(jnp.where(lane == idx, x, 0.0), axis=-1, keepdims=True)

    def body(n, carry):
        t0 = pl.multiple_of(n * c, c)
        mst = list(carry)
        qp = q_ref[pl.ds(t0, c), :]
        kp = k_ref[pl.ds(t0, c), :] * (ML_DQK ** -0.5)
        hm = [(lane >= j * ML_DQK) & (lane < (j + 1) * ML_DQK) for j in heads]
        qj = [jnp.where(hm[j], qp, 0.0).astype(BF16) for j in heads]
        kj = [jnp.where(hm[j], kp, 0.0) for j in heads]
        va = [jnp.concatenate([v_ref[pl.ds(t0, c), j * ML_DV:(j + 1) * ML_DV], ones_v], axis=1).astype(BF16)
              for j in heads]
        li = [take_lane(li_sc[pl.ds(t0, c), :], 2 * pair + j) for j in heads]
        bcum = [take_lane(bc_sc[pl.ds(t0, c), :], ML_HEADS + 2 * pair + j) for j in heads]
        dmat = [jnp.where(incl, bcum[j] - bcrow_sc[j, pl.ds(n, 1), :] + lirow_sc[j, pl.ds(n, 1), :], -jnp.inf)
                for j in heads]
        m_intra = [jnp.max(d, axis=-1, keepdims=True) for d in dmat]
        qk = [_bdot_nt(qj[j], kj[j]) for j in heads]
        inter = [_bdot(qj[j], cst_sc[j]) for j in heads]
        sqk = [qk[j] * jnp.exp(dmat[j] - m_intra[j]) for j in heads]
        intra = [_bdot(sqk[j], va[j]) for j in heads]
        bl = [b[c - 1:c, :] for b in bcum]
        wk = [bl[j] - bcum[j] + li[j] for j in heads]
        m_chunk = [jnp.max(w, axis=0, keepdims=True) for w in wk]
        kv = [_bdot_tn(kj[j] * jnp.exp(wk[j] - m_chunk[j]), va[j]) for j in heads]
        new_carry = []
        for j in heads:
            a_inter = bcum[j] + mst[j]
            m_t = jnp.maximum(a_inter, m_intra[j])
            s_inter = jnp.exp(a_inter - m_t)
            s_intra = jnp.exp(m_intra[j] - m_t)
            num = s_inter * inter[j][:, :ML_DV] + s_intra * intra[j][:, :ML_DV]
            den = s_inter * inter[j][:, ML_DV:] + s_intra * intra[j][:, ML_DV:]
            hout = num / jnp.maximum(jnp.abs(den), jnp.exp(-m_t))
            og = og_ref[pl.ds(t0, c), j * ML_DV:(j + 1) * ML_DV]
            o_ref[pl.ds(t0, c), j * ML_DV:(j + 1) * ML_DV] = (_rms(hout, onorm) * _sigmoid(og)).astype(o_ref.dtype)
            m_new = jnp.maximum(bl[j] + mst[j], m_chunk[j])
            fa = jnp.exp(bl[j] + mst[j] - m_new)
            fc = jnp.exp(m_chunk[j] - m_new)
            cst_sc[j] = fa * cst_sc[j] + fc * kv[j]
            new_carry.append(m_new)
        return tuple(new_carry)

    cst_sc[...] = jnp.zeros_like(cst_sc)
    lax.fori_loop(0, nchunk, body, (jnp.zeros((1, 1), F32),) * 2)


def _mlstm_mixer(proj, gates_t, i_bias, f_bias, out_norm, *, batch, seq, if_col):
    t = proj.shape[0]
    c = ML_CHUNK
    nchunk = seq // c
    hh = ML_HEADS
    npair = hh // 2
    kw = hh * ML_DQK
    vw = hh * ML_DV
    assert nchunk % 4 == 0
    g_t = gates_t.reshape(N_GATE_ROWS, batch, nchunk, c)
    bcast = lambda v: jnp.broadcast_to(v.astype(F32)[:, None], (hh, LANES))
    lanes = lambda v, off: jnp.pad(v.astype(F32), (off, LANES - hh - off)).reshape(1, LANES)
    fixed = lambda b, p: (0, 0)
    pw = 2 * ML_DV
    return pl.pallas_call(
        functools.partial(_mlstm_kernel, seq=seq),
        grid=(batch, npair),
        in_specs=[pl.BlockSpec((seq, LANES), lambda b, p: (b, p)),
                  pl.BlockSpec((seq, LANES), lambda b, p: (b, kw // LANES + p)),
                  pl.BlockSpec((seq, pw), lambda b, p: (b, 2 * kw // pw + p)),
                  pl.BlockSpec((seq, pw), lambda b, p: (b, (2 * kw + vw) // pw + p)),
                  pl.BlockSpec((seq, LANES), lambda b, p: (b, if_col // LANES)),
                  pl.BlockSpec((2, 1, nchunk, c), lambda b, p: (p, b, 0, 0)),
                  pl.BlockSpec((2, 1, nchunk, c), lambda b, p: (npair + p, b, 0, 0)),
                  pl.BlockSpec((1, LANES), fixed),
                  pl.BlockSpec((1, LANES), fixed),
                  pl.BlockSpec((hh, LANES), fixed),
                  pl.BlockSpec((hh, LANES), fixed),
                  pl.BlockSpec((1, LANES), fixed)],
        out_specs=pl.BlockSpec((seq, pw), lambda b, p: (b, p)),
        out_shape=jax.ShapeDtypeStruct((t, vw), BF16),
        scratch_shapes=[pltpu.VMEM((seq, LANES), F32),
                        pltpu.VMEM((seq, LANES), F32),
                        pltpu.VMEM((2, nchunk, c), F32),
                        pltpu.VMEM((2, nchunk, c), F32),
                        pltpu.VMEM((2, LANES, 2 * ML_DV), F32)],
        compiler_params=_params("arbitrary", "arbitrary"),
        name="mlstm_mixer",
    )(proj, proj, proj, proj, proj, g_t, g_t, lanes(i_bias, 0), lanes(f_bias, hh),
      bcast(i_bias), bcast(f_bias), out_norm.reshape(1, LANES))


def _sb_kernel(q_ref, k_ref, v_ref, o_ref, qm_sc, k_sc, vm_sc, mcat_sc, z_sc, pend_sc, carry_sc, acc_sc, *,
               seq):
    half = SB_BLOCK
    tile = 2 * half
    ntile = seq // tile
    lane = lax.broadcasted_iota(jnp.int32, (1, LANES), 1)
    head0 = lane < SB_DH
    q = q_ref[...] * (SB_DH ** -0.5 * LOG2E)
    qm_sc[0] = jnp.where(head0, q, 0.0).astype(BF16)
    qm_sc[1] = jnp.where(head0, 0.0, q).astype(BF16)
    k_sc[...] = k_ref[...].astype(BF16)
    v = v_ref[...]
    vm_sc[0] = jnp.where(head0, v, 0.0).astype(BF16)
    vm_sc[1] = jnp.where(head0, 0.0, v).astype(BF16)
    r = lax.broadcasted_iota(jnp.int32, (half, half), 0)
    c = lax.broadcasted_iota(jnp.int32, (half, half), 1)
    causal = c < r
    m_half = jnp.concatenate([(r > c).astype(BF16), jnp.ones((half, half), BF16)], axis=1)
    mcat_sc[...] = jnp.concatenate([m_half, m_half], axis=0)

    def split_rows(x, n):
        return [x[i * half:(i + 1) * half] for i in range(n)]

    def suffix_sums(blocks):
        hilo = []
        for lg in blocks:
            hi = lg.astype(BF16)
            lo = (lg - hi.astype(F32)).astype(BF16)
            hilo.append(jnp.concatenate([hi, lo], axis=1))
        sums = jnp.dot(jnp.concatenate(hilo, axis=0), mcat_sc[...], preferred_element_type=F32)
        return split_rows(sums, len(blocks))

    def scores(qms, kblk):
        z = lax.dot_general(jnp.concatenate(qms, axis=0), kblk, _NT, preferred_element_type=F32)
        return split_rows(z, len(qms))

    chains = range(4)

    def log_weights(z, modes):
        lg_l, lg_r, sums_l, sums_r = [], [], [], []
        for zz, m in zip(z, modes):
            lg = -(jnp.maximum(zz, 0.0) + jnp.log2(1.0 + jnp.exp2(-jnp.abs(zz))))
            ll = lg[:, :half]
            lr = None if m == "diag_first" else lg[:, half:]
            if m == "diag_first":
                ll = jnp.where(causal, ll, 0.0)
            if m == "diag_second":
                lr = jnp.where(causal, lr, 0.0)
            lg_l.append(ll)
            lg_r.append(lr)
            sums_r.append(None if lr is None else suffix_sums([lr])[0])
            sums_l.append(suffix_sums([ll])[0])
        return lg_l, lg_r, sums_l, sums_r

    def weights(ch, zz, ll, lr, sl_, sr, mode):
        if mode == "diag_first":
            att = jnp.where(causal, jnp.exp2(zz + ll + sl_[:, :half]), 0.0).astype(BF16)
            carry_sc[ch] = sl_[:, half:]
            return jnp.concatenate([att, jnp.zeros_like(att)], axis=1)
        if mode == "diag_second":
            carry_l, tail_r = sr[:, half:], sr[:, :half]
        else:
            cy = carry_sc[ch]
            carry_l, tail_r = cy + sr[:, half:], cy + sr[:, :half]
        att_r = jnp.exp2(zz[:, half:] + lr + tail_r)
        att_l = jnp.exp2(zz[:, :half] + ll + carry_l + sl_[:, :half])
        if mode == "diag_second":
            att_r = jnp.where(causal, att_r, 0.0)
        carry_sc[ch] = carry_l + sl_[:, half:]
        return jnp.concatenate([att_l.astype(BF16), att_r.astype(BF16)], axis=1)

    def vcat(s0, n):
        return jnp.concatenate([vm_sc[0, pl.ds(s0, n), :], vm_sc[1, pl.ds(s0, n), :]], axis=0)

    def weighted_values(slot, vv):
        pend = jnp.concatenate([pend_sc[slot, 0], pend_sc[slot, 1]], axis=0)
        return jnp.dot(pend, vv, preferred_element_type=F32)

    def key_tile(kb):
        return pl.multiple_of(jnp.maximum(kb, 0) * tile, tile)

    def q_tile(qb, _):
        t0 = pl.multiple_of(qb * tile, tile)
        qms = [qm_sc[j, pl.ds(t0 + rr * half, half), :] for rr in range(2) for j in range(2)]
        k_d = k_sc[pl.ds(t0, tile), :]
        modes = ["diag_first", "diag_first", "diag_second", "diag_second"]
        z = scores(qms[:2], k_d[:half]) + scores(qms[2:], k_d)
        parts = log_weights(z, modes)
        for ch in chains:
            rr, j = divmod(ch, 2)
            pend_sc[0, rr, :, j * tile:(j + 1) * tile] = weights(ch, z[ch], *[p[ch] for p in parts], modes[ch])
        acc_sc[...] = jnp.zeros_like(acc_sc)
        z_first = scores(qms, k_sc[pl.ds(key_tile(qb - 1), tile), :])
        for ch in chains:
            z_sc[0, ch] = z_first[ch]

        def inner(i, _):
            slot = i & 1
            kb = qb - 1 - i
            pv = split_rows(weighted_values(slot, vcat(key_tile(kb + 1), tile)), 2)
            z = [z_sc[slot, ch] for ch in chains]
            parts = log_weights(z, ["full"] * 4)
            z_next = scores(qms, k_sc[pl.ds(key_tile(kb - 1), tile), :])
            for ch in chains:
                z_sc[1 - slot, ch] = z_next[ch]
            for ch in chains:
                rr, j = divmod(ch, 2)
                pend_sc[1 - slot, rr, :, j * tile:(j + 1) * tile] = weights(
                    ch, z[ch], *[p[ch] for p in parts], "full")
            for rr in range(2):
                acc_sc[rr] += pv[rr]
            return 0

        lax.fori_loop(0, qb, inner, 0)
        pv = split_rows(weighted_values(qb & 1, vcat(0, tile)), 2)
        for rr in range(2):
            o_ref[pl.ds(t0 + rr * half, half), :] = (acc_sc[rr] + pv[rr]).astype(o_ref.dtype)
        return 0

    lax.fori_loop(0, ntile, q_tile, 0)


def _sb_mixer(proj, *, batch, seq):
    t = proj.shape[0]
    npair = SB_HEADS // 2
    w = SB_HEADS * SB_DH
    return pl.pallas_call(
        functools.partial(_sb_kernel, seq=seq),
        grid=(batch, npair),
        in_specs=[pl.BlockSpec((seq, LANES), lambda b, p: (b, p)),
                  pl.BlockSpec((seq, LANES), lambda b, p: (b, w // LANES + p)),
                  pl.BlockSpec((seq, LANES), lambda b, p: (b, 2 * w // LANES + p))],
        out_specs=pl.BlockSpec((seq, LANES), lambda b, p: (b, p)),
        out_shape=jax.ShapeDtypeStruct((t, w), BF16),
        scratch_shapes=[pltpu.VMEM((2, seq, LANES), BF16),
                        pltpu.VMEM((seq, LANES), BF16),
                        pltpu.VMEM((2, seq, LANES), BF16),
                        pltpu.VMEM((2 * SB_BLOCK, 2 * SB_BLOCK), BF16),
                        pltpu.VMEM((2, 4, SB_BLOCK, 2 * SB_BLOCK), F32),
                        pltpu.VMEM((2, 2, SB_BLOCK, 4 * SB_BLOCK), BF16),
                        pltpu.VMEM((4, SB_BLOCK, LANES), F32),
                        pltpu.VMEM((2, SB_BLOCK, LANES), F32)],
        compiler_params=_params("arbitrary", "arbitrary"),
        name="sb_mixer",
    )(proj, proj, proj)


def _pad_cols(w, width):
    return jnp.pad(w, ((0, 0), (0, width - w.shape[1])))


def kernel(x, mem, mem_norm, norm_pre_mix, norm_post_mix, norm_pre_ffn, norm_post_ffn, w_mem_kv, w_out,
           w_ffn_in, w_ffn_out, gdn_w_in, gdn_conv, gdn_a_log, gdn_dt_bias, gdn_out_norm, ml_w_in, ml_i_bias,
           ml_f_bias, ml_out_norm, sb_w_in):
    batch, seq, d = x.shape
    n_mem = mem.shape[1]
    depth = w_out.shape[0]
    d_ff = w_ffn_out.shape[1]
    h = x.reshape(batch * seq, d)
    mem2d = mem.reshape(batch * n_mem, d)
    tm = 512

    gdn_main = 3 * GDN_HEADS * GDN_DK + GDN_HEADS * LANES
    ml_main = 2 * ML_HEADS * ML_DQK + 2 * ML_HEADS * ML_DV
    sb_main = 3 * SB_HEADS * SB_DH

    for layer in range(depth):
        kind, j = layer % 3, layer // 3
        if kind == 0:
            w = gdn_w_in[j]
            gates = w[:, gdn_main:gdn_main + 2 * GDN_HEADS]
            w = jnp.concatenate([w[:, :gdn_main], w[:, gdn_main + 2 * GDN_HEADS:], _pad_cols(gates, LANES)], axis=1)
            main = gdn_main
        elif kind == 1:
            w = ml_w_in[j]
            gates = w[:, ml_main:ml_main + 2 * ML_HEADS]
            w = jnp.concatenate([w[:, :ml_main], w[:, ml_main + 2 * ML_HEADS:], _pad_cols(gates, LANES)], axis=1)
            main = ml_main
        else:
            w = sb_w_in[j]
            main = sb_main
        if kind == 0:
            proj, gates_t = _norm_matmul(h, norm_pre_mix[layer], w.astype(BF16), tm=tm, conv_w=gdn_conv[j],
                                         gate_col=main + MEM_W, seq=seq)
            ymix = _gdn_mixer(proj, gates_t, gdn_a_log[j], gdn_dt_bias[j], gdn_out_norm[j],
                              batch=batch, seq=seq, ab_col=main + MEM_W)
        elif kind == 1:
            proj, gates_t = _norm_matmul(h, norm_pre_mix[layer], w.astype(BF16), tm=tm, gate_col=main + MEM_W)
            ymix = _mlstm_mixer(proj, gates_t, ml_i_bias[j], ml_f_bias[j], ml_out_norm[j],
                                batch=batch, seq=seq, if_col=main + MEM_W)
        else:
            proj = _norm_matmul(h, norm_pre_mix[layer], w.astype(BF16), tm=tm)
            ymix = _sb_mixer(proj, batch=batch, seq=seq)
        kv = _norm_matmul(mem2d, mem_norm, w_mem_kv[layer].astype(BF16), tm=n_mem)
        h = _layer_tail(h, ymix, proj, kv, w_out[layer].astype(BF16), norm_post_mix[layer], norm_pre_ffn[layer],
                        w_ffn_in[layer].astype(BF16), w_ffn_out[layer].astype(BF16), norm_post_ffn[layer],
                        tm=tm, ff_chunk=256, seq=seq, n_mem=n_mem, qcol=main)
    return h.reshape(batch, seq, d)
```

```python
import functools

import jax
import jax.numpy as jnp
from jax import lax
from jax.experimental import pallas as pl
from jax.experimental.pallas import tpu as pltpu

F32 = jnp.float32
BF16 = jnp.bfloat16
EPS = 1e-6
LOG2E = 1.4426950408889634
HI = lax.Precision.HIGHEST

LANES = 128
VMEM_LIMIT_BYTES = 56 * 1024 * 1024

GDN_CHUNK = 128
GDN_GROUP = 4
ML_CHUNK = 128
GDN_HEADS = 8
GDN_DK = 128
ML_HEADS = 8
ML_DQK = 64
ML_DV = 128
SB_HEADS = 16
SB_DH = 64
SB_BLOCK = 128
MEM_HEADS = 4
MEM_DH = 128
MEM_W = MEM_HEADS * MEM_DH

_NT = (((1,), (1,)), ((), ()))
_TN = (((0,), (0,)), ((), ()))


def _params(*sem):
    return pltpu.CompilerParams(dimension_semantics=sem, vmem_limit_bytes=VMEM_LIMIT_BYTES)


def _rms(x, g):
    return x * lax.rsqrt(jnp.mean(x * x, axis=-1, keepdims=True) + EPS) * g


def _softplus(x):
    return jnp.maximum(x, 0.0) + jnp.log(1.0 + jnp.exp(-jnp.abs(x)))


def _sigmoid(x):
    return 1.0 / (1.0 + jnp.exp2(x * -LOG2E))


def _bdot(a, b):
    return jnp.dot(a.astype(BF16), b.astype(BF16), preferred_element_type=F32)


def _bdot_nt(a, b):
    return lax.dot_general(a.astype(BF16), b.astype(BF16), _NT, preferred_element_type=F32)


def _bdot_tn(a, b):
    return lax.dot_general(a.astype(BF16), b.astype(BF16), _TN, preferred_element_type=F32)


def _xdot(a, b):
    return jnp.dot(a, b, precision=HI, preferred_element_type=F32)


def _tril_cumsum(x):
    n = x.shape[0]
    tril = _tri(n, lower=True, strict=False).astype(BF16)
    hi = x.astype(BF16)
    r = x - hi.astype(F32)
    mid = r.astype(BF16)
    lo = (r - mid.astype(F32)).astype(BF16)
    return jnp.dot(jnp.concatenate([tril, tril, tril], axis=1), jnp.concatenate([hi, mid, lo], axis=0),
                   preferred_element_type=F32)


N_GATE_ROWS = 16


def _norm_matmul_kernel(*refs, col_chunk, conv_cols, gate_col, blocks_per_seq):
    x_ref, g_ref, w_ref = refs[:3]
    refs = list(refs[3:])
    cw_ref = refs.pop(0) if conv_cols else None
    o_ref = refs.pop(0)
    gt_ref = refs.pop(0) if gate_col is not None else None
    halo_sc = refs.pop(0) if conv_cols else None
    tm, n = o_ref.shape
    xn = _rms(x_ref[...], g_ref[...]).astype(BF16)
    if conv_cols:
        @pl.when(lax.rem(pl.program_id(0), blocks_per_seq) == 0)
        def _():
            halo_sc[...] = jnp.zeros_like(halo_sc)

    bounds = [(c0, min(c0 + col_chunk, n)) for c0 in range(0, n, col_chunk)]
    conv_chunks = [b for b in bounds if b[1] <= conv_cols]
    plain_chunks = [b for b in bounds if b[1] > conv_cols]
    order = []
    while conv_chunks or plain_chunks:
        order += conv_chunks[:1] + plain_chunks[:1]
        conv_chunks, plain_chunks = conv_chunks[1:], plain_chunks[1:]
    for c0, c1 in order:
        raw = jnp.dot(xn, w_ref[:, c0:c1], preferred_element_type=F32)
        if c1 <= conv_cols:
            width = c1 - c0
            xe = jnp.concatenate([halo_sc[:, c0:c1], raw], axis=0).reshape(tm // 8 + 1, 8, width)
            sub = lax.broadcasted_iota(jnp.int32, (1, 8, width), 1)
            cw = cw_ref[:, c0:c1]
            y = cw[3:4].reshape(1, 1, width) * xe[1:]
            for s in range(1, 4):
                rot = pltpu.roll(xe, s, axis=1)
                y = y + cw[3 - s:4 - s].reshape(1, 1, width) * jnp.where(sub >= s, rot[1:], rot[:-1])
            y = y.reshape(tm, width)
            o_ref[:, c0:c1] = y * _sigmoid(y)
            halo_sc[:, c0:c1] = raw[tm - 8:tm]
        else:
            o_ref[:, c0:c1] = raw
            if c0 == gate_col:
                gt_ref[...] = raw.T[:N_GATE_ROWS]


def _norm_matmul(x2d, g, w, *, tm, conv_w=None, gate_col=None, seq=None, col_chunk=512):
    t, d = x2d.shape
    n = w.shape[1]
    conv_cols = 0 if conv_w is None else conv_w.shape[1]
    assert conv_cols % col_chunk == 0 and (gate_col is None or (gate_col % col_chunk == 0 and n - gate_col == LANES))
    in_specs = [pl.BlockSpec((tm, d), lambda i: (i, 0)),
                pl.BlockSpec((1, d), lambda i: (0, 0)),
                pl.BlockSpec((d, n), lambda i: (0, 0))]
    args = [x2d, g.reshape(1, d), w]
    out_specs = [pl.BlockSpec((tm, n), lambda i: (i, 0))]
    out_shape = [jax.ShapeDtypeStruct((t, n), F32)]
    scratch = []
    if conv_cols:
        in_specs.append(pl.BlockSpec(conv_w.shape, lambda i: (0, 0)))
        args.append(conv_w)
        scratch.append(pltpu.VMEM((8, conv_cols), F32))
    if gate_col is not None:
        out_specs.append(pl.BlockSpec((N_GATE_ROWS, tm), lambda i: (0, i)))
        out_shape.append(jax.ShapeDtypeStruct((N_GATE_ROWS, t), F32))
    out = pl.pallas_call(
        functools.partial(_norm_matmul_kernel, col_chunk=col_chunk, conv_cols=conv_cols, gate_col=gate_col,
                          blocks_per_seq=None if seq is None else seq // tm),
        grid=(t // tm,),
        in_specs=in_specs,
        out_specs=out_specs,
        out_shape=out_shape,
        scratch_shapes=scratch,
        compiler_params=_params("arbitrary"),
        name="norm_matmul",
    )(*args)
    return out if gate_col is not None else out[0]


def _mem_attention(q_ref, k_ref, v_ref):
    scale = MEM_DH ** -0.5
    heads = []
    for h in range(MEM_HEADS):
        sl = slice(h * MEM_DH, (h + 1) * MEM_DH)
        s = _bdot_nt(q_ref[:, sl], k_ref[:, sl]) * scale
        e = jnp.exp(s - jnp.max(s, axis=-1, keepdims=True))
        p = e / jnp.sum(e, axis=-1, keepdims=True)
        heads.append(_bdot(p, v_ref[:, sl]).astype(BF16))
    return jnp.concatenate(heads, axis=1)


def _tail_kernel(h_ref, ymix_ref, q_ref, k_ref, v_ref, wo1_ref, wo2_ref, gpm_ref, gpf_ref, win_ref, wout_ref,
                 gqf_ref, o_ref, *, d_ff, ff_chunk):
    ymem = _mem_attention(q_ref, k_ref, v_ref)
    y = _bdot(ymix_ref[...], wo1_ref[...]) + jnp.dot(ymem, wo2_ref[...], preferred_element_type=F32)
    h1 = h_ref[...] + _rms(y, gpm_ref[...])
    n = _rms(h1, gpf_ref[...]).astype(BF16)
    acc = jnp.zeros_like(h1)
    for c0 in range(0, d_ff, ff_chunk):
        g = jnp.dot(n, win_ref[:, c0:c0 + ff_chunk], preferred_element_type=F32)
        u = jnp.dot(n, win_ref[:, d_ff + c0:d_ff + c0 + ff_chunk], preferred_element_type=F32)
        a = (g * _sigmoid(g) * u).astype(BF16)
        acc = acc + jnp.dot(a, wout_ref[c0:c0 + ff_chunk, :], preferred_element_type=F32)
    o_ref[...] = h1 + _rms(acc, gqf_ref[...])


def _layer_tail(h2d, ymix, proj, kv, wo, g_post_mix, g_pre_ffn, w_in, w_out, g_post_ffn, *, tm, ff_chunk, seq,
                n_mem, qcol):
    t, d = h2d.shape
    mixw = ymix.shape[1]
    d_ff = w_out.shape[0]
    blocks_per_seq = seq // tm
    row = lambda i: (i, 0)
    fixed = lambda i: (0, 0)
    return pl.pallas_call(
        functools.partial(_tail_kernel, d_ff=d_ff, ff_chunk=ff_chunk),
        grid=(t // tm,),
        in_specs=[pl.BlockSpec((tm, d), row),
                  pl.BlockSpec((tm, mixw), row),
                  pl.BlockSpec((tm, MEM_W), lambda i: (i, qcol // MEM_W)),
                  pl.BlockSpec((n_mem, MEM_W), lambda i: (i // blocks_per_seq, 0)),
                  pl.BlockSpec((n_mem, MEM_W), lambda i: (i // blocks_per_seq, 1)),
                  pl.BlockSpec((mixw, d), fixed),
                  pl.BlockSpec((MEM_W, d), fixed),
                  pl.BlockSpec((1, d), fixed),
                  pl.BlockSpec((1, d), fixed),
                  pl.BlockSpec((d, 2 * d_ff), fixed),
                  pl.BlockSpec((d_ff, d), fixed),
                  pl.BlockSpec((1, d), fixed)],
        out_specs=pl.BlockSpec((tm, d), row),
        out_shape=jax.ShapeDtypeStruct((t, d), F32),
        compiler_params=_params("arbitrary"),
        name="layer_tail",
    )(h2d, ymix, proj, kv, kv, wo[:mixw], wo[mixw:], g_post_mix.reshape(1, d), g_pre_ffn.reshape(1, d),
      w_in, w_out, g_post_ffn.reshape(1, d))


def _tri(n, lower, strict):
    r = lax.broadcasted_iota(jnp.int32, (n, n), 0)
    c = lax.broadcasted_iota(jnp.int32, (n, n), 1)
    if lower:
        return (r > c) if strict else (r >= c)
    return (r < c) if strict else (r <= c)


def _gdn_kernel(q_ref, k_ref, v_ref, z_ref, ab_ref, at_ref, alane_ref, dlane_ref,
                alog_ref, dtb_ref, onorm_ref, o_ref, gall_sc, beta_sc, grow_sc, u_sc, wq_sc, kg_sc, p_sc, gl_sc,
                state_sc, *, seq):
    c = GDN_CHUNK
    nchunk = seq // c
    gsz = GDN_GROUP
    ngroup = nchunk // gsz
    pair = pl.program_id(1)
    lane = lax.broadcasted_iota(jnp.int32, (1, LANES), 1)
    incl = _tri(c, lower=True, strict=False)
    strict = _tri(c, lower=True, strict=True)
    rxc = lax.broadcasted_iota(jnp.int32, (c, c), 0) ^ lax.broadcasted_iota(jnp.int32, (c, c), 1)
    onorm = onorm_ref[...]

    @pl.when(pair == 0)
    def _():
        ab = ab_ref[...]
        beta_sc[...] = _sigmoid(ab)
        gall_sc[...] = -jnp.exp(alane_ref[...]) * _softplus(ab + dlane_ref[...])

        def cum_body(i, _):
            t0 = [pl.multiple_of((4 * i + d) * c, c) for d in range(4)]
            cum = [_tril_cumsum(gall_sc[pl.ds(t, c), :]) for t in t0]
            for t, g in zip(t0, cum):
                gall_sc[pl.ds(t, c), :] = g
            return 0

        lax.fori_loop(0, nchunk // 4, cum_body, 0)

    triu_f = _tri(c, lower=False, strict=False).astype(F32)
    for j in range(2):
        head = 2 * pair + j
        neg_a = -jnp.exp(alog_ref[pl.ds(head, 1), :])
        g_row = neg_a * _softplus(at_ref[j, 0] + dtb_ref[pl.ds(head, 1), :])
        grow_sc[j] = _xdot(g_row, triu_f)

    def l2n(x):
        return x * lax.rsqrt(jnp.sum(x * x, axis=-1, keepdims=True) + EPS)

    def take_lane(x, idx):
        return jnp.sum(jnp.where(lane == idx, x, 0.0), axis=-1, keepdims=True)

    heads = range(2)

    def prep(n, dn, j):
        t0 = pl.multiple_of(n * c, c)
        head = 2 * pair + j
        sl = slice(j * LANES, (j + 1) * LANES)
        qc = l2n(q_ref[pl.ds(t0, c), sl]) * (GDN_DK ** -0.5)
        kc = l2n(k_ref[pl.ds(t0, c), sl])
        vc = v_ref[pl.ds(t0, c), sl]
        gcol = take_lane(gall_sc[pl.ds(t0, c), :], head)
        beta = take_lane(beta_sc[pl.ds(t0, c), :], GDN_HEADS + head)
        g_row = grow_sc[j, pl.ds(n, 1), :]
        decay = jnp.where(incl, jnp.exp(jnp.where(incl, gcol - g_row, 0.0)), 0.0)
        return dict(dn=dn, j=j, qc=qc, kc=kc, vc=vc, gcol=gcol, beta=beta, decay=decay)

    def local_work(g, slot):
        ch = []
        for dn in range(gsz):
            for j in heads:
                ch.append(prep(gsz * g + dn, dn, j))
                yield
        kq = [_bdot_nt(jnp.concatenate([x["kc"], x["qc"]], axis=0), x["kc"]) for x in ch]
        a_mat = [jnp.where(strict, x["beta"] * m[:c] * x["decay"], 0.0) for x, m in zip(ch, kq)]
        nm = [-jnp.where(rxc == 1, a, 0.0) for a in a_mat]
        yield
        for lvl in range(1, c.bit_length() - 1):
            mask = lax.shift_right_logical(rxc, lvl) == 1
            a_l = [jnp.where(mask, a, 0.0) for a in a_mat]
            y = [_bdot(m, a) for m, a in zip(nm, a_l)]
            yield
            ay = [a + yy for a, yy in zip(a_l, y)]
            z = [_bdot(a, m) for a, m in zip(ay, nm)]
            yield
            nm = [m - a - zz for m, a, zz in zip(nm, ay, z)]
        eg = [jnp.exp(x["gcol"]) for x in ch]
        rhs = [jnp.concatenate([x["vc"] * x["beta"], x["kc"] * (x["beta"] * e)], axis=1) for x, e in zip(ch, eg)]
        sol = [r + _bdot(m, r) for r, m in zip(rhs, nm)]
        yield
        for x, s, e, m in zip(ch, sol, eg, kq):
            j, dn = x["j"], x["dn"]
            g_last = x["gcol"][c - 1:c, :]
            u_sc[slot, j, dn] = s[:, :LANES]
            wq_sc[slot, j, dn, 0:c, :] = s[:, LANES:].astype(BF16)
            wq_sc[slot, j, dn, c:2 * c, :] = (x["qc"] * e).astype(BF16)
            kg_sc[slot, j, dn] = (x["kc"] * jnp.exp(g_last - x["gcol"])).astype(BF16)
            p_sc[slot, j, dn] = (m[c:] * x["decay"]).astype(BF16)
            gl_sc[slot, j, dn] = jnp.broadcast_to(jnp.exp(g_last), (1, LANES))

    def recurrence(g, slot):
        for dn in range(gsz):
            t0 = pl.multiple_of(jnp.maximum(gsz * g + dn, 0) * c, c)
            ws = [_bdot(wq_sc[slot, j, dn], state_sc[j]) for j in heads]
            yield
            v_new = [u_sc[slot, j, dn] - ws[j][:c] for j in heads]
            upd = [_bdot_tn(kg_sc[slot, j, dn], v_new[j]) for j in heads]
            o = [ws[j][c:] + _bdot(p_sc[slot, j, dn], v_new[j]) for j in heads]
            yield
            for j in heads:
                sl = slice(j * LANES, (j + 1) * LANES)
                z = z_ref[pl.ds(t0, c), sl]
                o_ref[pl.ds(t0, c), sl] = (_rms(o[j], onorm) * (z * _sigmoid(z))).astype(o_ref.dtype)
                state_sc[j] = state_sc[j] * gl_sc[slot, j, dn] + upd[j]
            yield

    def interleave(main, side, every):
        k, main_done, side_done = 0, False, False
        while not (main_done and side_done):
            if not main_done:
                main_done = next(main, "end") == "end"
            k += 1
            if not side_done and (main_done or k % every == 0):
                side_done = next(side, "end") == "end"

    for ref in (u_sc, wq_sc, kg_sc, p_sc, gl_sc):
        ref[1] = jnp.zeros(ref.shape[1:], ref.dtype)
    state_sc[...] = jnp.zeros_like(state_sc)

    def double_trip(d, _):
        interleave(local_work(2 * d, 0), recurrence(2 * d - 1, 1), every=2)
        interleave(local_work(2 * d + 1, 1), recurrence(2 * d, 0), every=2)
        return 0

    lax.fori_loop(0, ngroup // 2, double_trip, 0)
    for _ in recurrence(ngroup - 1, 1):
        pass


def _gdn_mixer(proj, gates_t, a_log, dt_bias, out_norm, *, batch, seq, ab_col):
    t = proj.shape[0]
    c = GDN_CHUNK
    nchunk = seq // c
    hh = GDN_HEADS
    npair = hh // 2
    pw = 2 * LANES
    gsz = GDN_GROUP
    assert nchunk % (2 * gsz) == 0
    a_t = gates_t.reshape(N_GATE_ROWS, batch, nchunk, c)
    bcast = lambda v: jnp.broadcast_to(v.astype(F32)[:, None], (hh, LANES))
    lanes = lambda v: jnp.pad(v.astype(F32), (0, LANES - hh)).reshape(1, LANES)
    col = lambda off: (lambda b, p: (b, off + p))
    fixed = lambda b, p: (0, 0)
    return pl.pallas_call(
        functools.partial(_gdn_kernel, seq=seq),
        grid=(batch, npair),
        in_specs=[pl.BlockSpec((seq, pw), col(0)),
                  pl.BlockSpec((seq, pw), col(npair)),
                  pl.BlockSpec((seq, pw), col(2 * npair)),
                  pl.BlockSpec((seq, pw), col(3 * npair)),
                  pl.BlockSpec((seq, LANES), lambda b, p: (b, ab_col // LANES)),
                  pl.BlockSpec((2, 1, nchunk, c), lambda b, p: (p, b, 0, 0)),
                  pl.BlockSpec((1, LANES), fixed),
                  pl.BlockSpec((1, LANES), fixed),
                  pl.BlockSpec((hh, LANES), fixed),
                  pl.BlockSpec((hh, LANES), fixed),
                  pl.BlockSpec((1, LANES), fixed)],
        out_specs=pl.BlockSpec((seq, pw), col(0)),
        out_shape=jax.ShapeDtypeStruct((t, hh * LANES), BF16),
        scratch_shapes=[pltpu.VMEM((seq, LANES), F32),
                        pltpu.VMEM((seq, LANES), F32),
                        pltpu.VMEM((2, nchunk, c), F32),
                        pltpu.VMEM((2, 2, gsz, c, LANES), F32),
                        pltpu.VMEM((2, 2, gsz, 2 * c, LANES), BF16),
                        pltpu.VMEM((2, 2, gsz, c, LANES), BF16),
                        pltpu.VMEM((2, 2, gsz, c, c), BF16),
                        pltpu.VMEM((2, 2, gsz, 1, LANES), F32),
                        pltpu.VMEM((2, GDN_DK, LANES), F32)],
        compiler_params=_params("arbitrary", "arbitrary"),
        name="gdn_mixer",
    )(proj, proj, proj, proj, proj, a_t, lanes(a_log), lanes(dt_bias),
      bcast(a_log), bcast(dt_bias), out_norm.reshape(1, LANES))


def _mlstm_kernel(q_ref, k_ref, v_ref, og_ref, if_ref, it_ref, ft_ref, ilane_ref, flane_ref, ib_ref, fb_ref,
                  onorm_ref, o_ref, li_sc, bc_sc, lirow_sc, bcrow_sc, cst_sc, *, seq):
    c = ML_CHUNK
    nchunk = seq // c
    pair = pl.program_id(1)
    heads = range(2)
    incl = _tri(c, lower=True, strict=False)
    triu_f = _tri(c, lower=False, strict=False).astype(F32)
    lane = lax.broadcasted_iota(jnp.int32, (1, LANES), 1)
    onorm = onorm_ref[...]
    ones_v = jnp.ones((c, ML_DV), F32)

    @pl.when(pair == 0)
    def _():
        gates = if_ref[...]
        li_sc[...] = gates + ilane_ref[...]
        bc_sc[...] = -_softplus(-(gates + flane_ref[...]))

        def cum_body(i, _):
            t0 = [pl.multiple_of((4 * i + d) * c, c) for d in range(4)]
            cum = [_tril_cumsum(bc_sc[pl.ds(t, c), :]) for t in t0]
            for t, b in zip(t0, cum):
                bc_sc[pl.ds(t, c), :] = b
            return 0

        lax.fori_loop(0, nchunk // 4, cum_body, 0)

    for j in heads:
        head = 2 * pair + j
        lirow_sc[j] = it_ref[j, 0] + ib_ref[pl.ds(head, 1), :]
        bcrow_sc[j] = _xdot(-_softplus(-(ft_ref[j, 0] + fb_ref[pl.ds(head, 1), :])), triu_f)

    def take_lane(x, idx):
        return jnp.sum(jnp.where(lane == idx, x, 0.0), axis=-1, keepdims=True)

    def body(n, carry):
        t0 = pl.multiple_of(n * c, c)
        mst = list(carry)
        qp = q_ref[pl.ds(t0, c), :]
        kp = k_ref[pl.ds(t0, c), :] * (ML_DQK ** -0.5)
        hm = [(lane >= j * ML_DQK) & (lane < (j + 1) * ML_DQK) for j in heads]
        qj = [jnp.where(hm[j], qp, 0.0).astype(BF16) for j in heads]
        kj = [jnp.where(hm[j], kp, 0.0) for j in heads]
        va = [jnp.concatenate([v_ref[pl.ds(t0, c), j * ML_DV:(j + 1) * ML_DV], ones_v], axis=1).astype(BF16)
              for j in heads]
        li = [take_lane(li_sc[pl.ds(t0, c), :], 2 * pair + j) for j in heads]
        bcum = [take_lane(bc_sc[pl.ds(t0, c), :], ML_HEADS + 2 * pair + j) for j in heads]
        dmat = [jnp.where(incl, bcum[j] - bcrow_sc[j, pl.ds(n, 1), :] + lirow_sc[j, pl.ds(n, 1), :], -jnp.inf)
                for j in heads]
        m_intra = [jnp.max(d, axis=-1, keepdims=True) for d in dmat]
        qk = [_bdot_nt(qj[j], kj[j]) for j in heads]
        inter = [_bdot(qj[j], cst_sc[j]) for j in heads]
        sqk = [qk[j] * jnp.exp(dmat[j] - m_intra[j]) for j in heads]
        intra = [_bdot(sqk[j], va[j]) for j in heads]
        bl = [b[c - 1:c, :] for b in bcum]
        wk = [bl[j] - bcum[j] + li[j] for j in heads]
        m_chunk = [jnp.max(w, axis=0, keepdims=True) for w in wk]
        kv = [_bdot_tn(kj[j] * jnp.exp(wk[j] - m_chunk[j]), va[j]) for j in heads]
        new_carry = []
        for j in heads:
            a_inter = bcum[j] + mst[j]
            m_t = jnp.maximum(a_inter, m_intra[j])
            s_inter = jnp.exp(a_inter - m_t)
            s_intra = jnp.exp(m_intra[j] - m_t)
            num = s_inter * inter[j][:, :ML_DV] + s_intra * intra[j][:, :ML_DV]
            den = s_inter * inter[j][:, ML_DV:] + s_intra * intra[j][:, ML_DV:]
            hout = num / jnp.maximum(jnp.abs(den), jnp.exp(-m_t))
            og = og_ref[pl.ds(t0, c), j * ML_DV:(j + 1) * ML_DV]
            o_ref[pl.ds(t0, c), j * ML_DV:(j + 1) * ML_DV] = (_rms(hout, onorm) * _sigmoid(og)).astype(o_ref.dtype)
            m_new = jnp.maximum(bl[j] + mst[j], m_chunk[j])
            fa = jnp.exp(bl[j] + mst[j] - m_new)
            fc = jnp.exp(m_chunk[j] - m_new)
            cst_sc[j] = fa * cst_sc[j] + fc * kv[j]
            new_carry.append(m_new)
        return tuple(new_carry)

    cst_sc[...] = jnp.zeros_like(cst_sc)
    lax.fori_loop(0, nchunk, body, (jnp.zeros((1, 1), F32),) * 2)


def _mlstm_mixer(proj, gates_t, i_bias, f_bias, out_norm, *, batch, seq, if_col):
    t = proj.shape[0]
    c = ML_CHUNK
    nchunk = seq // c
    hh = ML_HEADS
    npair = hh // 2
    kw = hh * ML_DQK
    vw = hh * ML_DV
    assert nchunk % 4 == 0
    g_t = gates_t.reshape(N_GATE_ROWS, batch, nchunk, c)
    bcast = lambda v: jnp.broadcast_to(v.astype(F32)[:, None], (hh, LANES))
    lanes = lambda v, off: jnp.pad(v.astype(F32), (off, LANES - hh - off)).reshape(1, LANES)
    fixed = lambda b, p: (0, 0)
    pw = 2 * ML_DV
    return pl.pallas_call(
        functools.partial(_mlstm_kernel, seq=seq),
        grid=(batch, npair),
        in_specs=[pl.BlockSpec((seq, LANES), lambda b, p: (b, p)),
                  pl.BlockSpec((seq, LANES), lambda b, p: (b, kw // LANES + p)),
                  pl.BlockSpec((seq, pw), lambda b, p: (b, 2 * kw // pw + p)),
                  pl.BlockSpec((seq, pw), lambda b, p: (b, (2 * kw + vw) // pw + p)),
                  pl.BlockSpec((seq, LANES), lambda b, p: (b, if_col // LANES)),
                  pl.BlockSpec((2, 1, nchunk, c), lambda b, p: (p, b, 0, 0)),
                  pl.BlockSpec((2, 1, nchunk, c), lambda b, p: (npair + p, b, 0, 0)),
                  pl.BlockSpec((1, LANES), fixed),
                  pl.BlockSpec((1, LANES), fixed),
                  pl.BlockSpec((hh, LANES), fixed),
                  pl.BlockSpec((hh, LANES), fixed),
                  pl.BlockSpec((1, LANES), fixed)],
        out_specs=pl.BlockSpec((seq, pw), lambda b, p: (b, p)),
        out_shape=jax.ShapeDtypeStruct((t, vw), BF16),
        scratch_shapes=[pltpu.VMEM((seq, LANES), F32),
                        pltpu.VMEM((seq, LANES), F32),
                        pltpu.VMEM((2, nchunk, c), F32),
                        pltpu.VMEM((2, nchunk, c), F32),
                        pltpu.VMEM((2, LANES, 2 * ML_DV), F32)],
        compiler_params=_params("arbitrary", "arbitrary"),
        name="mlstm_mixer",
    )(proj, proj, proj, proj, proj, g_t, g_t, lanes(i_bias, 0), lanes(f_bias, hh),
      bcast(i_bias), bcast(f_bias), out_norm.reshape(1, LANES))


def _sb_kernel(q_ref, k_ref, v_ref, o_ref, qm_sc, k_sc, vm_sc, mcat_sc, z_sc, pend_sc, carry_sc, acc_sc, *,
               seq):
    half = SB_BLOCK
    tile = 2 * half
    ntile = seq // tile
    lane = lax.broadcasted_iota(jnp.int32, (1, LANES), 1)
    head0 = lane < SB_DH
    q = q_ref[...] * (SB_DH ** -0.5 * LOG2E)
    qm_sc[0] = jnp.where(head0, q, 0.0).astype(BF16)
    qm_sc[1] = jnp.where(head0, 0.0, q).astype(BF16)
    k_sc[...] = k_ref[...].astype(BF16)
    v = v_ref[...]
    vm_sc[0] = jnp.where(head0, v, 0.0).astype(BF16)
    vm_sc[1] = jnp.where(head0, 0.0, v).astype(BF16)
    r = lax.broadcasted_iota(jnp.int32, (half, half), 0)
    c = lax.broadcasted_iota(jnp.int32, (half, half), 1)
    causal = c < r
    m_half = jnp.concatenate([(r > c).astype(BF16), jnp.ones((half, half), BF16)], axis=1)
    mcat_sc[...] = jnp.concatenate([m_half, m_half], axis=0)

    def split_rows(x, n):
        return [x[i * half:(i + 1) * half] for i in range(n)]

    def suffix_sums(blocks):
        hilo = []
        for lg in blocks:
            hi = lg.astype(BF16)
            lo = (lg - hi.astype(F32)).astype(BF16)
            hilo.append(jnp.concatenate([hi, lo], axis=1))
        sums = jnp.dot(jnp.concatenate(hilo, axis=0), mcat_sc[...], preferred_element_type=F32)
        return split_rows(sums, len(blocks))

    def scores(qms, kblk):
        z = lax.dot_general(jnp.concatenate(qms, axis=0), kblk, _NT, preferred_element_type=F32)
        return split_rows(z, len(qms))

    chains = range(4)

    def log_weights(z, modes):
        lg_l, lg_r, sums_l, sums_r = [], [], [], []
        for zz, m in zip(z, modes):
            lg = -(jnp.maximum(zz, 0.0) + jnp.log2(1.0 + jnp.exp2(-jnp.abs(zz))))
            ll = lg[:, :half]
            lr = None if m == "diag_first" else lg[:, half:]
            if m == "diag_first":
                ll = jnp.where(causal, ll, 0.0)
            if m == "diag_second":
                lr = jnp.where(causal, lr, 0.0)
            lg_l.append(ll)
            lg_r.append(lr)
            sums_r.append(None if lr is None else suffix_sums([lr])[0])
            sums_l.append(suffix_sums([ll])[0])
        return lg_l, lg_r, sums_l, sums_r

    def weights(ch, zz, ll, lr, sl_, sr, mode):
        if mode == "diag_first":
            att = jnp.where(causal, jnp.exp2(zz + ll + sl_[:, :half]), 0.0).astype(BF16)
            carry_sc[ch] = sl_[:, half:]
            return jnp.concatenate([att, jnp.zeros_like(att)], axis=1)
        if mode == "diag_second":
            carry_l, tail_r = sr[:, half:], sr[:, :half]
        else:
            cy = carry_sc[ch]
            carry_l, tail_r = cy + sr[:, half:], cy + sr[:, :half]
        att_r = jnp.exp2(zz[:, half:] + lr + tail_r)
        att_l = jnp.exp2(zz[:, :half] + ll + carry_l + sl_[:, :half])
        if mode == "diag_second":
            att_r = jnp.where(causal, att_r, 0.0)
        carry_sc[ch] = carry_l + sl_[:, half:]
        return jnp.concatenate([att_l.astype(BF16), att_r.astype(BF16)], axis=1)

    def vcat(s0, n):
        return jnp.concatenate([vm_sc[0, pl.ds(s0, n), :], vm_sc[1, pl.ds(s0, n), :]], axis=0)

    def weighted_values(slot, vv):
        pend = jnp.concatenate([pend_sc[slot, 0], pend_sc[slot, 1]], axis=0)
        return jnp.dot(pend, vv, preferred_element_type=F32)

    def key_tile(kb):
        return pl.multiple_of(jnp.maximum(kb, 0) * tile, tile)

    def q_tile(qb, _):
        t0 = pl.multiple_of(qb * tile, tile)
        qms = [qm_sc[j, pl.ds(t0 + rr * half, half), :] for rr in range(2) for j in range(2)]
        k_d = k_sc[pl.ds(t0, tile), :]
        modes = ["diag_first", "diag_first", "diag_second", "diag_second"]
        z = scores(qms[:2], k_d[:half]) + scores(qms[2:], k_d)
        parts = log_weights(z, modes)
        for ch in chains:
            rr, j = divmod(ch, 2)
            pend_sc[0, rr, :, j * tile:(j + 1) * tile] = weights(ch, z[ch], *[p[ch] for p in parts], modes[ch])
        acc_sc[...] = jnp.zeros_like(acc_sc)
        z_first = scores(qms, k_sc[pl.ds(key_tile(qb - 1), tile), :])
        for ch in chains:
            z_sc[0, ch] = z_first[ch]

        def inner(i, _):
            slot = i & 1
            kb = qb - 1 - i
            pv = split_rows(weighted_values(slot, vcat(key_tile(kb + 1), tile)), 2)
            z = [z_sc[slot, ch] for ch in chains]
            parts = log_weights(z, ["full"] * 4)
            z_next = scores(qms, k_sc[pl.ds(key_tile(kb - 1), tile), :])
            for ch in chains:
                z_sc[1 - slot, ch] = z_next[ch]
            for ch in chains:
                rr, j = divmod(ch, 2)
                pend_sc[1 - slot, rr, :, j * tile:(j + 1) * tile] = weights(
                    ch, z[ch], *[p[ch] for p in parts], "full")
            for rr in range(2):
                acc_sc[rr] += pv[rr]
            return 0

        lax.fori_loop(0, qb, inner, 0)
        pv = split_rows(weighted_values(qb & 1, vcat(0, tile)), 2)
        for rr in range(2):
            o_ref[pl.ds(t0 + rr * half, half), :] = (acc_sc[rr] + pv[rr]).astype(o_ref.dtype)
        return 0

    lax.fori_loop(0, ntile, q_tile, 0)


def _sb_mixer(proj, *, batch, seq):
    t = proj.shape[0]
    npair = SB_HEADS // 2
    w = SB_HEADS * SB_DH
    return pl.pallas_call(
        functools.partial(_sb_kernel, seq=seq),
        grid=(batch, npair),
        in_specs=[pl.BlockSpec((seq, LANES), lambda b, p: (b, p)),
                  pl.BlockSpec((seq, LANES), lambda b, p: (b, w // LANES + p)),
                  pl.BlockSpec((seq, LANES), lambda b, p: (b, 2 * w // LANES + p))],
        out_specs=pl.BlockSpec((seq, LANES), lambda b, p: (b, p)),
        out_shape=jax.ShapeDtypeStruct((t, w), BF16),
        scratch_shapes=[pltpu.VMEM((2, seq, LANES), BF16),
                        pltpu.VMEM((seq, LANES), BF16),
                        pltpu.VMEM((2, seq, LANES), BF16),
                        pltpu.VMEM((2 * SB_BLOCK, 2 * SB_BLOCK), BF16),
                        pltpu.VMEM((2, 4, SB_BLOCK, 2 * SB_BLOCK), F32),
                        pltpu.VMEM((2, 2, SB_BLOCK, 4 * SB_BLOCK), BF16),
                        pltpu.VMEM((4, SB_BLOCK, LANES), F32),
                        pltpu.VMEM((2, SB_BLOCK, LANES), F32)],
        compiler_params=_params("arbitrary", "arbitrary"),
        name="sb_mixer",
    )(proj, proj, proj)


def _pad_cols(w, width):
    return jnp.pad(w, ((0, 0), (0, width - w.shape[1])))


def kernel(x, mem, mem_norm, norm_pre_mix, norm_post_mix, norm_pre_ffn, norm_post_ffn, w_mem_kv, w_out,
           w_ffn_in, w_ffn_out, gdn_w_in, gdn_conv, gdn_a_log, gdn_dt_bias, gdn_out_norm, ml_w_in, ml_i_bias,
           ml_f_bias, ml_out_norm, sb_w_in):
    batch, seq, d = x.shape
    n_mem = mem.shape[1]
    depth = w_out.shape[0]
    d_ff = w_ffn_out.shape[1]
    h = x.reshape(batch * seq, d)
    mem2d = mem.reshape(batch * n_mem, d)
    tm = 512

    gdn_main = 3 * GDN_HEADS * GDN_DK + GDN_HEADS * LANES
    ml_main = 2 * ML_HEADS * ML_DQK + 2 * ML_HEADS * ML_DV
    sb_main = 3 * SB_HEADS * SB_DH

    for layer in range(depth):
        kind, j = layer % 3, layer // 3
        if kind == 0:
            w = gdn_w_in[j].astype(BF16)
            gates = w[:, gdn_main:gdn_main + 2 * GDN_HEADS]
            w = jnp.concatenate([w[:, :gdn_main], w[:, gdn_main + 2 * GDN_HEADS:], _pad_cols(gates, LANES)], axis=1)
            main = gdn_main
        elif kind == 1:
            w = ml_w_in[j].astype(BF16)
            gates = w[:, ml_main:ml_main + 2 * ML_HEADS]
            w = jnp.concatenate([w[:, :ml_main], w[:, ml_main + 2 * ML_HEADS:], _pad_cols(gates, LANES)], axis=1)
            main = ml_main
        else:
            w = sb_w_in[j].astype(BF16)
            main = sb_main
        if kind == 0:
            proj, gates_t = _norm_matmul(h, norm_pre_mix[layer], w, tm=tm, conv_w=gdn_conv[j],
                                         gate_col=main + MEM_W, seq=seq)
            ymix = _gdn_mixer(proj, gates_t, gdn_a_log[j], gdn_dt_bias[j], gdn_out_norm[j],
                              batch=batch, seq=seq, ab_col=main + MEM_W)
        elif kind == 1:
            proj, gates_t = _norm_matmul(h, norm_pre_mix[layer], w, tm=tm, gate_col=main + MEM_W)
            ymix = _mlstm_mixer(proj, gates_t, ml_i_bias[j], ml_f_bias[j], ml_out_norm[j],
                                batch=batch, seq=seq, if_col=main + MEM_W)
        else:
            proj = _norm_matmul(h, norm_pre_mix[layer], w, tm=tm)
            ymix = _sb_mixer(proj, batch=batch, seq=seq)
        kv = _norm_matmul(mem2d, mem_norm, w_mem_kv[layer].astype(BF16), tm=n_mem)
        h = _layer_tail(h, ymix, proj, kv, w_out[layer].astype(BF16), norm_post_mix[layer], norm_pre_ffn[layer],
                        w_ffn_in[layer].astype(BF16), w_ffn_out[layer].astype(BF16), norm_post_ffn[layer],
                        tm=tm, ff_chunk=256, seq=seq, n_mem=n_mem, qcol=main)
    return h.reshape(batch, seq, d)
```

```python
import functools

import jax
import jax.numpy as jnp
from jax import lax
from jax.experimental import pallas as pl
from jax.experimental.pallas import tpu as pltpu

F32 = jnp.float32
BF16 = jnp.bfloat16
EPS = 1e-6
LOG2E = 1.4426950408889634
HI = lax.Precision.HIGHEST

LANES = 128
VMEM_LIMIT_BYTES = 56 * 1024 * 1024

GDN_CHUNK = 128
GDN_GROUP = 4
ML_CHUNK = 128
GDN_HEADS = 8
GDN_DK = 128
ML_HEADS = 8
ML_DQK = 64
ML_DV = 128
SB_HEADS = 16
SB_DH = 64
SB_BLOCK = 128
MEM_HEADS = 4
MEM_DH = 128
MEM_W = MEM_HEADS * MEM_DH

_NT = (((1,), (1,)), ((), ()))
_TN = (((0,), (0,)), ((), ()))


def _params(*sem):
    return pltpu.CompilerParams(dimension_semantics=sem, vmem_limit_bytes=VMEM_LIMIT_BYTES)


def _rms(x, g):
    return x * lax.rsqrt(jnp.mean(x * x, axis=-1, keepdims=True) + EPS) * g


def _softplus(x):
    return jnp.maximum(x, 0.0) + jnp.log(1.0 + jnp.exp(-jnp.abs(x)))


def _sigmoid(x):
    return 1.0 / (1.0 + jnp.exp2(x * -LOG2E))


def _bdot(a, b):
    return jnp.dot(a.astype(BF16), b.astype(BF16), preferred_element_type=F32)


def _bdot_nt(a, b):
    return lax.dot_general(a.astype(BF16), b.astype(BF16), _NT, preferred_element_type=F32)


def _bdot_tn(a, b):
    return lax.dot_general(a.astype(BF16), b.astype(BF16), _TN, preferred_element_type=F32)


def _xdot(a, b):
    return jnp.dot(a, b, precision=HI, preferred_element_type=F32)


def _tril_cumsum(x):
    n = x.shape[0]
    tril = _tri(n, lower=True, strict=False).astype(BF16)
    hi = x.astype(BF16)
    r = x - hi.astype(F32)
    mid = r.astype(BF16)
    lo = (r - mid.astype(F32)).astype(BF16)
    return jnp.dot(jnp.concatenate([tril, tril, tril], axis=1), jnp.concatenate([hi, mid, lo], axis=0),
                   preferred_element_type=F32)


N_GATE_ROWS = 16


def _norm_matmul_kernel(*refs, col_chunk, conv_cols, gate_col, blocks_per_seq):
    x_ref, g_ref, w_ref = refs[:3]
    refs = list(refs[3:])
    cw_ref = refs.pop(0) if conv_cols else None
    o_ref = refs.pop(0)
    gt_ref = refs.pop(0) if gate_col is not None else None
    halo_sc = refs.pop(0) if conv_cols else None
    tm, n = o_ref.shape
    xn = _rms(x_ref[...], g_ref[...]).astype(BF16)
    if conv_cols:
        @pl.when(lax.rem(pl.program_id(0), blocks_per_seq) == 0)
        def _():
            halo_sc[...] = jnp.zeros_like(halo_sc)

    bounds = [(c0, min(c0 + col_chunk, n)) for c0 in range(0, n, col_chunk)]
    conv_chunks = [b for b in bounds if b[1] <= conv_cols]
    plain_chunks = [b for b in bounds if b[1] > conv_cols]
    order = []
    while conv_chunks or plain_chunks:
        order += conv_chunks[:1] + plain_chunks[:1]
        conv_chunks, plain_chunks = conv_chunks[1:], plain_chunks[1:]
    for c0, c1 in order:
        raw = jnp.dot(xn, w_ref[:, c0:c1], preferred_element_type=F32)
        if c1 <= conv_cols:
            width = c1 - c0
            xe = jnp.concatenate([halo_sc[:, c0:c1], raw], axis=0).reshape(tm // 8 + 1, 8, width)
            sub = lax.broadcasted_iota(jnp.int32, (1, 8, width), 1)
            cw = cw_ref[:, c0:c1]
            y = cw[3:4].reshape(1, 1, width) * xe[1:]
            for s in range(1, 4):
                rot = pltpu.roll(xe, s, axis=1)
                y = y + cw[3 - s:4 - s].reshape(1, 1, width) * jnp.where(sub >= s, rot[1:], rot[:-1])
            y = y.reshape(tm, width)
            o_ref[:, c0:c1] = y * _sigmoid(y)
            halo_sc[:, c0:c1] = raw[tm - 8:tm]
        else:
            o_ref[:, c0:c1] = raw
            if c0 == gate_col:
                gt_ref[...] = raw.T[:N_GATE_ROWS]


def _norm_matmul(x2d, g, w, *, tm, conv_w=None, gate_col=None, seq=None, col_chunk=512):
    t, d = x2d.shape
    n = w.shape[1]
    conv_cols = 0 if conv_w is None else conv_w.shape[1]
    assert conv_cols % col_chunk == 0 and (gate_col is None or (gate_col % col_chunk == 0 and n - gate_col == LANES))
    in_specs = [pl.BlockSpec((tm, d), lambda i: (i, 0)),
                pl.BlockSpec((1, d), lambda i: (0, 0)),
                pl.BlockSpec((d, n), lambda i: (0, 0))]
    args = [x2d, g.reshape(1, d), w]
    out_specs = [pl.BlockSpec((tm, n), lambda i: (i, 0))]
    out_shape = [jax.ShapeDtypeStruct((t, n), F32)]
    scratch = []
    if conv_cols:
        in_specs.append(pl.BlockSpec(conv_w.shape, lambda i: (0, 0)))
        args.append(conv_w)
        scratch.append(pltpu.VMEM((8, conv_cols), F32))
    if gate_col is not None:
        out_specs.append(pl.BlockSpec((N_GATE_ROWS, tm), lambda i: (0, i)))
        out_shape.append(jax.ShapeDtypeStruct((N_GATE_ROWS, t), F32))
    out = pl.pallas_call(
        functools.partial(_norm_matmul_kernel, col_chunk=col_chunk, conv_cols=conv_cols, gate_col=gate_col,
                          blocks_per_seq=None if seq is None else seq // tm),
        grid=(t // tm,),
        in_specs=in_specs,
        out_specs=out_specs,
        out_shape=out_shape,
        scratch_shapes=scratch,
        compiler_params=_params("arbitrary"),
        name="norm_matmul",
    )(*args)
    return out if gate_col is not None else out[0]


def _mem_attention(q_ref, k_ref, v_ref):
    scale = MEM_DH ** -0.5
    heads = []
    for h in range(MEM_HEADS):
        sl = slice(h * MEM_DH, (h + 1) * MEM_DH)
        s = _bdot_nt(q_ref[:, sl], k_ref[:, sl]) * scale
        e = jnp.exp(s - jnp.max(s, axis=-1, keepdims=True))
        p = e / jnp.sum(e, axis=-1, keepdims=True)
        heads.append(_bdot(p, v_ref[:, sl]).astype(BF16))
    return jnp.concatenate(heads, axis=1)


def _tail_kernel(h_ref, ymix_ref, q_ref, k_ref, v_ref, wo1_ref, wo2_ref, gpm_ref, gpf_ref, win_ref, wout_ref,
                 gqf_ref, o_ref, *, d_ff, ff_chunk):
    ymem = _mem_attention(q_ref, k_ref, v_ref)
    y = _bdot(ymix_ref[...], wo1_ref[...]) + jnp.dot(ymem, wo2_ref[...], preferred_element_type=F32)
    h1 = h_ref[...] + _rms(y, gpm_ref[...])
    n = _rms(h1, gpf_ref[...]).astype(BF16)
    acc = jnp.zeros_like(h1)
    for c0 in range(0, d_ff, ff_chunk):
        g = jnp.dot(n, win_ref[:, c0:c0 + ff_chunk], preferred_element_type=F32)
        u = jnp.dot(n, win_ref[:, d_ff + c0:d_ff + c0 + ff_chunk], preferred_element_type=F32)
        a = (g * _sigmoid(g) * u).astype(BF16)
        acc = acc + jnp.dot(a, wout_ref[c0:c0 + ff_chunk, :], preferred_element_type=F32)
    o_ref[...] = h1 + _rms(acc, gqf_ref[...])


def _layer_tail(h2d, ymix, proj, kv, wo, g_post_mix, g_pre_ffn, w_in, w_out, g_post_ffn, *, tm, ff_chunk, seq,
                n_mem, qcol):
    t, d = h2d.shape
    mixw = ymix.shape[1]
    d_ff = w_out.shape[0]
    blocks_per_seq = seq // tm
    row = lambda i: (i, 0)
    fixed = lambda i: (0, 0)
    return pl.pallas_call(
        functools.partial(_tail_kernel, d_ff=d_ff, ff_chunk=ff_chunk),
        grid=(t // tm,),
        in_specs=[pl.BlockSpec((tm, d), row),
                  pl.BlockSpec((tm, mixw), row),
                  pl.BlockSpec((tm, MEM_W), lambda i: (i, qcol // MEM_W)),
                  pl.BlockSpec((n_mem, MEM_W), lambda i: (i // blocks_per_seq, 0)),
                  pl.BlockSpec((n_mem, MEM_W), lambda i: (i // blocks_per_seq, 1)),
                  pl.BlockSpec((mixw, d), fixed),
                  pl.BlockSpec((MEM_W, d), fixed),
                  pl.BlockSpec((1, d), fixed),
                  pl.BlockSpec((1, d), fixed),
                  pl.BlockSpec((d, 2 * d_ff), fixed),
                  pl.BlockSpec((d_ff, d), fixed),
                  pl.BlockSpec((1, d), fixed)],
        out_specs=pl.BlockSpec((tm, d), row),
        out_shape=jax.ShapeDtypeStruct((t, d), F32),
        compiler_params=_params("arbitrary"),
        name="layer_tail",
    )(h2d, ymix, proj, kv, kv, wo[:mixw], wo[mixw:], g_post_mix.reshape(1, d), g_pre_ffn.reshape(1, d),
      w_in, w_out, g_post_ffn.reshape(1, d))


def _tri(n, lower, strict):
    r = lax.broadcasted_iota(jnp.int32, (n, n), 0)
    c = lax.broadcasted_iota(jnp.int32, (n, n), 1)
    if lower:
        return (r > c) if strict else (r >= c)
    return (r < c) if strict else (r <= c)


def _gdn_kernel(q_ref, k_ref, v_ref, z_ref, ab_ref, at_ref, alane_ref, dlane_ref,
                alog_ref, dtb_ref, onorm_ref, o_ref, gall_sc, beta_sc, grow_sc, u_sc, wq_sc, kg_sc, p_sc, gl_sc,
                state_sc, *, seq):
    c = GDN_CHUNK
    nchunk = seq // c
    gsz = GDN_GROUP
    ngroup = nchunk // gsz
    pair = pl.program_id(1)
    lane = lax.broadcasted_iota(jnp.int32, (1, LANES), 1)
    incl = _tri(c, lower=True, strict=False)
    strict = _tri(c, lower=True, strict=True)
    rxc = lax.broadcasted_iota(jnp.int32, (c, c), 0) ^ lax.broadcasted_iota(jnp.int32, (c, c), 1)
    onorm = onorm_ref[...]

    @pl.when(pair == 0)
    def _():
        ab = ab_ref[...]
        beta_sc[...] = _sigmoid(ab)
        gall_sc[...] = -jnp.exp(alane_ref[...]) * _softplus(ab + dlane_ref[...])

        def cum_body(i, _):
            t0 = [pl.multiple_of((4 * i + d) * c, c) for d in range(4)]
            cum = [_tril_cumsum(gall_sc[pl.ds(t, c), :]) for t in t0]
            for t, g in zip(t0, cum):
                gall_sc[pl.ds(t, c), :] = g
            return 0

        lax.fori_loop(0, nchunk // 4, cum_body, 0)

    triu_f = _tri(c, lower=False, strict=False).astype(F32)
    for j in range(2):
        head = 2 * pair + j
        neg_a = -jnp.exp(alog_ref[pl.ds(head, 1), :])
        g_row = neg_a * _softplus(at_ref[j, 0] + dtb_ref[pl.ds(head, 1), :])
        grow_sc[j] = _xdot(g_row, triu_f)

    def l2n(x):
        return x * lax.rsqrt(jnp.sum(x * x, axis=-1, keepdims=True) + EPS)

    def take_lane(x, idx):
        return jnp.sum(jnp.where(lane == idx, x, 0.0), axis=-1, keepdims=True)

    heads = range(2)

    def prep(n, dn, j):
        t0 = pl.multiple_of(n * c, c)
        head = 2 * pair + j
        sl = slice(j * LANES, (j + 1) * LANES)
        qc = l2n(q_ref[pl.ds(t0, c), sl]) * (GDN_DK ** -0.5)
        kc = l2n(k_ref[pl.ds(t0, c), sl])
        vc = v_ref[pl.ds(t0, c), sl]
        gcol = take_lane(gall_sc[pl.ds(t0, c), :], head)
        beta = take_lane(beta_sc[pl.ds(t0, c), :], GDN_HEADS + head)
        g_row = grow_sc[j, pl.ds(n, 1), :]
        decay = jnp.where(incl, jnp.exp(jnp.where(incl, gcol - g_row, 0.0)), 0.0)
        return dict(dn=dn, j=j, qc=qc, kc=kc, vc=vc, gcol=gcol, beta=beta, decay=decay)

    def local_work(g, slot):
        ch = []
        for dn in range(gsz):
            for j in heads:
                ch.append(prep(gsz * g + dn, dn, j))
                yield
        kq = [_bdot_nt(jnp.concatenate([x["kc"], x["qc"]], axis=0), x["kc"]) for x in ch]
        a_mat = [jnp.where(strict, x["beta"] * m[:c] * x["decay"], 0.0) for x, m in zip(ch, kq)]
        nm = [-jnp.where(rxc == 1, a, 0.0) for a in a_mat]
        yield
        for lvl in range(1, c.bit_length() - 1):
            mask = lax.shift_right_logical(rxc, lvl) == 1
            a_l = [jnp.where(mask, a, 0.0) for a in a_mat]
            y = [_bdot(m, a) for m, a in zip(nm, a_l)]
            yield
            ay = [a + yy for a, yy in zip(a_l, y)]
            z = [_bdot(a, m) for a, m in zip(ay, nm)]
            yield
            nm = [m - a - zz for m, a, zz in zip(nm, ay, z)]
        eg = [jnp.exp(x["gcol"]) for x in ch]
        rhs = [jnp.concatenate([x["vc"] * x["beta"], x["kc"] * (x["beta"] * e)], axis=1) for x, e in zip(ch, eg)]
        sol = [r + _bdot(m, r) for r, m in zip(rhs, nm)]
        yield
        for x, s, e, m in zip(ch, sol, eg, kq):
            j, dn = x["j"], x["dn"]
            g_last = x["gcol"][c - 1:c, :]
            u_sc[slot, j, dn] = s[:, :LANES]
            wq_sc[slot, j, dn, 0:c, :] = s[:, LANES:].astype(BF16)
            wq_sc[slot, j, dn, c:2 * c, :] = (x["qc"] * e).astype(BF16)
            kg_sc[slot, j, dn] = (x["kc"] * jnp.exp(g_last - x["gcol"])).astype(BF16)
            p_sc[slot, j, dn] = (m[c:] * x["decay"]).astype(BF16)
            gl_sc[slot, j, dn] = jnp.broadcast_to(jnp.exp(g_last), (1, LANES))

    def recurrence(g, slot):
        for dn in range(gsz):
            t0 = pl.multiple_of(jnp.maximum(gsz * g + dn, 0) * c, c)
            ws = [_bdot(wq_sc[slot, j, dn], state_sc[j]) for j in heads]
            yield
            v_new = [u_sc[slot, j, dn] - ws[j][:c] for j in heads]
            upd = [_bdot_tn(kg_sc[slot, j, dn], v_new[j]) for j in heads]
            o = [ws[j][c:] + _bdot(p_sc[slot, j, dn], v_new[j]) for j in heads]
            yield
            for j in heads:
                sl = slice(j * LANES, (j + 1) * LANES)
                z = z_ref[pl.ds(t0, c), sl]
                o_ref[pl.ds(t0, c), sl] = (_rms(o[j], onorm) * (z * _sigmoid(z))).astype(o_ref.dtype)
                state_sc[j] = state_sc[j] * gl_sc[slot, j, dn] + upd[j]
            yield

    def interleave(main, side, every):
        k, main_done, side_done = 0, False, False
        while not (main_done and side_done):
            if not main_done:
                main_done = next(main, "end") == "end"
            k += 1
            if not side_done and (main_done or k % every == 0):
                side_done = next(side, "end") == "end"

    for ref in (u_sc, wq_sc, kg_sc, p_sc, gl_sc):
        ref[1] = jnp.zeros(ref.shape[1:], ref.dtype)
    state_sc[...] = jnp.zeros_like(state_sc)

    def double_trip(d, _):
        interleave(local_work(2 * d, 0), recurrence(2 * d - 1, 1), every=2)
        interleave(local_work(2 * d + 1, 1), recurrence(2 * d, 0), every=2)
        return 0

    lax.fori_loop(0, ngroup // 2, double_trip, 0)
    for _ in recurrence(ngroup - 1, 1):
        pass


def _gdn_mixer(proj, gates_t, a_log, dt_bias, out_norm, *, batch, seq, ab_col):
    t = proj.shape[0]
    c = GDN_CHUNK
    nchunk = seq // c
    hh = GDN_HEADS
    npair = hh // 2
    pw = 2 * LANES
    gsz = GDN_GROUP
    assert nchunk % (2 * gsz) == 0
    a_t = gates_t.reshape(N_GATE_ROWS, batch, nchunk, c)
    bcast = lambda v: jnp.broadcast_to(v.astype(F32)[:, None], (hh, LANES))
    lanes = lambda v: jnp.pad(v.astype(F32), (0, LANES - hh)).reshape(1, LANES)
    col = lambda off: (lambda b, p: (b, off + p))
    fixed = lambda b, p: (0, 0)
    return pl.pallas_call(
        functools.partial(_gdn_kernel, seq=seq),
        grid=(batch, npair),
        in_specs=[pl.BlockSpec((seq, pw), col(0)),
                  pl.BlockSpec((seq, pw), col(npair)),
                  pl.BlockSpec((seq, pw), col(2 * npair)),
                  pl.BlockSpec((seq, pw), col(3 * npair)),
                  pl.BlockSpec((seq, LANES), lambda b, p: (b, ab_col // LANES)),
                  pl.BlockSpec((2, 1, nchunk, c), lambda b, p: (p, b, 0, 0)),
                  pl.BlockSpec((1, LANES), fixed),
                  pl.BlockSpec((1, LANES), fixed),
                  pl.BlockSpec((hh, LANES), fixed),
                  pl.BlockSpec((hh, LANES), fixed),
                  pl.BlockSpec((1, LANES), fixed)],
        out_specs=pl.BlockSpec((seq, pw), col(0)),
        out_shape=jax.ShapeDtypeStruct((t, hh * LANES), BF16),
        scratch_shapes=[pltpu.VMEM((seq, LANES), F32),
                        pltpu.VMEM((seq, LANES), F32),
                        pltpu.VMEM((2, nchunk, c), F32),
                        pltpu.VMEM((2, 2, gsz, c, LANES), F32),
                        pltpu.VMEM((2, 2, gsz, 2 * c, LANES), BF16),
                        pltpu.VMEM((2, 2, gsz, c, LANES), BF16),
                        pltpu.VMEM((2, 2, gsz, c, c), BF16),
                        pltpu.VMEM((2, 2, gsz, 1, LANES), F32),
                        pltpu.VMEM((2, GDN_DK, LANES), F32)],
        compiler_params=_params("arbitrary", "arbitrary"),
        name="gdn_mixer",
    )(proj, proj, proj, proj, proj, a_t, lanes(a_log), lanes(dt_bias),
      bcast(a_log), bcast(dt_bias), out_norm.reshape(1, LANES))


def _mlstm_kernel(q_ref, k_ref, v_ref, og_ref, if_ref, it_ref, ft_ref, ilane_ref, flane_ref, ib_ref, fb_ref,
                  onorm_ref, o_ref, li_sc, bc_sc, lirow_sc, bcrow_sc, cst_sc, *, seq):
    c = ML_CHUNK
    nchunk = seq // c
    pair = pl.program_id(1)
    heads = range(2)
    incl = _tri(c, lower=True, strict=False)
    triu_f = _tri(c, lower=False, strict=False).astype(F32)
    lane = lax.broadcasted_iota(jnp.int32, (1, LANES), 1)
    onorm = onorm_ref[...]
    ones_v = jnp.ones((c, ML_DV), F32)

    @pl.when(pair == 0)
    def _():
        gates = if_ref[...]
        li_sc[...] = gates + ilane_ref[...]
        bc_sc[...] = -_softplus(-(gates + flane_ref[...]))

        def cum_body(i, _):
            t0 = [pl.multiple_of((4 * i + d) * c, c) for d in range(4)]
            cum = [_tril_cumsum(bc_sc[pl.ds(t, c), :]) for t in t0]
            for t, b in zip(t0, cum):
                bc_sc[pl.ds(t, c), :] = b
            return 0

        lax.fori_loop(0, nchunk // 4, cum_body, 0)

    for j in heads:
        head = 2 * pair + j
        lirow_sc[j] = it_ref[j, 0] + ib_ref[pl.ds(head, 1), :]
        bcrow_sc[j] = _xdot(-_softplus(-(ft_ref[j, 0] + fb_ref[pl.ds(head, 1), :])), triu_f)

    def take_lane(x, idx):
        return jnp.sum(jnp.where(lane == idx, x, 0.0), axis=-1, keepdims=True)

    def body(i, carry):
        mst = list(carry)
        ns = [2 * i, 2 * i + 1]
        t0s = [pl.multiple_of(n * c, c) for n in ns]
        chs = [(dn, j) for dn in range(2) for j in heads]
        hm = [(lane >= j * ML_DQK) & (lane < (j + 1) * ML_DQK) for j in heads]
        qp = [q_ref[pl.ds(t0, c), :] for t0 in t0s]
        kp = [k_ref[pl.ds(t0, c), :] * (ML_DQK ** -0.5) for t0 in t0s]
        qj = {(dn, j): jnp.where(hm[j], qp[dn], 0.0).astype(BF16) for dn, j in chs}
        kj = {(dn, j): jnp.where(hm[j], kp[dn], 0.0) for dn, j in chs}
        va = {(dn, j): jnp.concatenate([v_ref[pl.ds(t0s[dn], c), j * ML_DV:(j + 1) * ML_DV], ones_v],
                                       axis=1).astype(BF16) for dn, j in chs}
        li = {(dn, j): take_lane(li_sc[pl.ds(t0s[dn], c), :], 2 * pair + j) for dn, j in chs}
        bcum = {(dn, j): take_lane(bc_sc[pl.ds(t0s[dn], c), :], ML_HEADS + 2 * pair + j) for dn, j in chs}
        dmat = {(dn, j): jnp.where(incl, bcum[dn, j] - bcrow_sc[j, pl.ds(ns[dn], 1), :]
                                   + lirow_sc[j, pl.ds(ns[dn], 1), :], -jnp.inf) for dn, j in chs}
        m_intra = {k: jnp.max(d, axis=-1, keepdims=True) for k, d in dmat.items()}
        qk = {k: _bdot_nt(qj[k], kj[k]) for k in chs}
        sqk = {k: qk[k] * jnp.exp(dmat[k] - m_intra[k]) for k in chs}
        intra = {k: _bdot(sqk[k], va[k]) for k in chs}
        bl = {k: bcum[k][c - 1:c, :] for k in chs}
        wk = {k: bl[k] - bcum[k] + li[k] for k in chs}
        m_chunk = {k: jnp.max(wk[k], axis=0, keepdims=True) for k in chs}
        kv = {k: _bdot_tn(kj[k] * jnp.exp(wk[k] - m_chunk[k]), va[k]) for k in chs}
        for dn in range(2):
            inter = [_bdot(qj[dn, j], cst_sc[j]) for j in heads]
            for j in heads:
                k = (dn, j)
                a_inter = bcum[k] + mst[j]
                m_t = jnp.maximum(a_inter, m_intra[k])
                s_inter = jnp.exp(a_inter - m_t)
                s_intra = jnp.exp(m_intra[k] - m_t)
                num = s_inter * inter[j][:, :ML_DV] + s_intra * intra[k][:, :ML_DV]
                den = s_inter * inter[j][:, ML_DV:] + s_intra * intra[k][:, ML_DV:]
                hout = num / jnp.maximum(jnp.abs(den), jnp.exp(-m_t))
                og = og_ref[pl.ds(t0s[dn], c), j * ML_DV:(j + 1) * ML_DV]
                o_ref[pl.ds(t0s[dn], c), j * ML_DV:(j + 1) * ML_DV] = (
                    _rms(hout, onorm) * _sigmoid(og)).astype(o_ref.dtype)
                m_new = jnp.maximum(bl[k] + mst[j], m_chunk[k])
                fa = jnp.exp(bl[k] + mst[j] - m_new)
                fc = jnp.exp(m_chunk[k] - m_new)
                cst_sc[j] = fa * cst_sc[j] + fc * kv[k]
                mst[j] = m_new
        return tuple(mst)

    cst_sc[...] = jnp.zeros_like(cst_sc)
    lax.fori_loop(0, nchunk // 2, body, (jnp.zeros((1, 1), F32),) * 2)


def _mlstm_mixer(proj, gates_t, i_bias, f_bias, out_norm, *, batch, seq, if_col):
    t = proj.shape[0]
    c = ML_CHUNK
    nchunk = seq // c
    hh = ML_HEADS
    npair = hh // 2
    kw = hh * ML_DQK
    vw = hh * ML_DV
    assert nchunk % 4 == 0
    g_t = gates_t.reshape(N_GATE_ROWS, batch, nchunk, c)
    bcast = lambda v: jnp.broadcast_to(v.astype(F32)[:, None], (hh, LANES))
    lanes = lambda v, off: jnp.pad(v.astype(F32), (off, LANES - hh - off)).reshape(1, LANES)
    fixed = lambda b, p: (0, 0)
    pw = 2 * ML_DV
    return pl.pallas_call(
        functools.partial(_mlstm_kernel, seq=seq),
        grid=(batch, npair),
        in_specs=[pl.BlockSpec((seq, LANES), lambda b, p: (b, p)),
                  pl.BlockSpec((seq, LANES), lambda b, p: (b, kw // LANES + p)),
                  pl.BlockSpec((seq, pw), lambda b, p: (b, 2 * kw // pw + p)),
                  pl.BlockSpec((seq, pw), lambda b, p: (b, (2 * kw + vw) // pw + p)),
                  pl.BlockSpec((seq, LANES), lambda b, p: (b, if_col // LANES)),
                  pl.BlockSpec((2, 1, nchunk, c), lambda b, p: (p, b, 0, 0)),
                  pl.BlockSpec((2, 1, nchunk, c), lambda b, p: (npair + p, b, 0, 0)),
                  pl.BlockSpec((1, LANES), fixed),
                  pl.BlockSpec((1, LANES), fixed),
                  pl.BlockSpec((hh, LANES), fixed),
                  pl.BlockSpec((hh, LANES), fixed),
                  pl.BlockSpec((1, LANES), fixed)],
        out_specs=pl.BlockSpec((seq, pw), lambda b, p: (b, p)),
        out_shape=jax.ShapeDtypeStruct((t, vw), BF16),
        scratch_shapes=[pltpu.VMEM((seq, LANES), F32),
                        pltpu.VMEM((seq, LANES), F32),
                        pltpu.VMEM((2, nchunk, c), F32),
                        pltpu.VMEM((2, nchunk, c), F32),
                        pltpu.VMEM((2, LANES, 2 * ML_DV), F32)],
        compiler_params=_params("arbitrary", "arbitrary"),
        name="mlstm_mixer",
    )(proj, proj, proj, proj, proj, g_t, g_t, lanes(i_bias, 0), lanes(f_bias, hh),
      bcast(i_bias), bcast(f_bias), out_norm.reshape(1, LANES))


def _sb_kernel(q_ref, k_ref, v_ref, o_ref, qm_sc, k_sc, vm_sc, mcat_sc, z_sc, pend_sc, carry_sc, acc_sc, *,
               seq):
    half = SB_BLOCK
    tile = 2 * half
    ntile = seq // tile
    lane = lax.broadcasted_iota(jnp.int32, (1, LANES), 1)
    head0 = lane < SB_DH
    q = q_ref[...] * (SB_DH ** -0.5 * LOG2E)
    qm_sc[0] = jnp.where(head0, q, 0.0).astype(BF16)
    qm_sc[1] = jnp.where(head0, 0.0, q).astype(BF16)
    k_sc[...] = k_ref[...].astype(BF16)
    v = v_ref[...]
    vm_sc[0] = jnp.where(head0, v, 0.0).astype(BF16)
    vm_sc[1] = jnp.where(head0, 0.0, v).astype(BF16)
    r = lax.broadcasted_iota(jnp.int32, (half, half), 0)
    c = lax.broadcasted_iota(jnp.int32, (half, half), 1)
    causal = c < r
    m_half = jnp.concatenate([(r > c).astype(BF16), jnp.ones((half, half), BF16)], axis=1)
    mcat_sc[...] = jnp.concatenate([m_half, m_half], axis=0)

    def split_rows(x, n):
        return [x[i * half:(i + 1) * half] for i in range(n)]

    def suffix_sums(blocks):
        hilo = []
        for lg in blocks:
            hi = lg.astype(BF16)
            lo = (lg - hi.astype(F32)).astype(BF16)
            hilo.append(jnp.concatenate([hi, lo], axis=1))
        sums = jnp.dot(jnp.concatenate(hilo, axis=0), mcat_sc[...], preferred_element_type=F32)
        return split_rows(sums, len(blocks))

    def scores(qms, kblk):
        z = lax.dot_general(jnp.concatenate(qms, axis=0), kblk, _NT, preferred_element_type=F32)
        return split_rows(z, len(qms))

    chains = range(4)

    def log_weights(z, modes):
        lg_l, lg_r, sums_l, sums_r = [], [], [], []
        for zz, m in zip(z, modes):
            lg = -(jnp.maximum(zz, 0.0) + jnp.log2(1.0 + jnp.exp2(-jnp.abs(zz))))
            ll = lg[:, :half]
            lr = None if m == "diag_first" else lg[:, half:]
            if m == "diag_first":
                ll = jnp.where(causal, ll, 0.0)
            if m == "diag_second":
                lr = jnp.where(causal, lr, 0.0)
            lg_l.append(ll)
            lg_r.append(lr)
            sums_r.append(None if lr is None else suffix_sums([lr])[0])
            sums_l.append(suffix_sums([ll])[0])
        return lg_l, lg_r, sums_l, sums_r

    def weights(ch, zz, ll, lr, sl_, sr, mode):
        if mode == "diag_first":
            att = jnp.where(causal, jnp.exp2(zz + ll + sl_[:, :half]), 0.0).astype(BF16)
            carry_sc[ch] = sl_[:, half:]
            return jnp.concatenate([att, jnp.zeros_like(att)], axis=1)
        if mode == "diag_second":
            carry_l, tail_r = sr[:, half:], sr[:, :half]
        else:
            cy = carry_sc[ch]
            carry_l, tail_r = cy + sr[:, half:], cy + sr[:, :half]
        att_r = jnp.exp2(zz[:, half:] + lr + tail_r)
        att_l = jnp.exp2(zz[:, :half] + ll + carry_l + sl_[:, :half])
        if mode == "diag_second":
            att_r = jnp.where(causal, att_r, 0.0)
        carry_sc[ch] = carry_l + sl_[:, half:]
        return jnp.concatenate([att_l.astype(BF16), att_r.astype(BF16)], axis=1)

    def vcat(s0, n):
        return jnp.concatenate([vm_sc[0, pl.ds(s0, n), :], vm_sc[1, pl.ds(s0, n), :]], axis=0)

    def weighted_values(slot, vv):
        pend = jnp.concatenate([pend_sc[slot, 0], pend_sc[slot, 1]], axis=0)
        return jnp.dot(pend, vv, preferred_element_type=F32)

    def key_tile(kb):
        return pl.multiple_of(jnp.maximum(kb, 0) * tile, tile)

    def q_tile(qb, _):
        t0 = pl.multiple_of(qb * tile, tile)
        qms = [qm_sc[j, pl.ds(t0 + rr * half, half), :] for rr in range(2) for j in range(2)]
        k_d = k_sc[pl.ds(t0, tile), :]
        modes = ["diag_first", "diag_first", "diag_second", "diag_second"]
        z = scores(qms[:2], k_d[:half]) + scores(qms[2:], k_d)
        parts = log_weights(z, modes)
        for ch in chains:
            rr, j = divmod(ch, 2)
            pend_sc[0, rr, :, j * tile:(j + 1) * tile] = weights(ch, z[ch], *[p[ch] for p in parts], modes[ch])
        acc_sc[...] = jnp.zeros_like(acc_sc)
        z_first = scores(qms, k_sc[pl.ds(key_tile(qb - 1), tile), :])
        for ch in chains:
            z_sc[0, ch] = z_first[ch]

        def inner(i, _):
            slot = i & 1
            kb = qb - 1 - i
            pv = split_rows(weighted_values(slot, vcat(key_tile(kb + 1), tile)), 2)
            z = [z_sc[slot, ch] for ch in chains]
            parts = log_weights(z, ["full"] * 4)
            z_next = scores(qms, k_sc[pl.ds(key_tile(kb - 1), tile), :])
            for ch in chains:
                z_sc[1 - slot, ch] = z_next[ch]
            for ch in chains:
                rr, j = divmod(ch, 2)
                pend_sc[1 - slot, rr, :, j * tile:(j + 1) * tile] = weights(
                    ch, z[ch], *[p[ch] for p in parts], "full")
            for rr in range(2):
                acc_sc[rr] += pv[rr]
            return 0

        lax.fori_loop(0, qb, inner, 0)
        pv = split_rows(weighted_values(qb & 1, vcat(0, tile)), 2)
        for rr in range(2):
            o_ref[pl.ds(t0 + rr * half, half), :] = (acc_sc[rr] + pv[rr]).astype(o_ref.dtype)
        return 0

    lax.fori_loop(0, ntile, q_tile, 0)


def _sb_mixer(proj, *, batch, seq):
    t = proj.shape[0]
    npair = SB_HEADS // 2
    w = SB_HEADS * SB_DH
    return pl.pallas_call(
        functools.partial(_sb_kernel, seq=seq),
        grid=(batch, npair),
        in_specs=[pl.BlockSpec((seq, LANES), lambda b, p: (b, p)),
                  pl.BlockSpec((seq, LANES), lambda b, p: (b, w // LANES + p)),
                  pl.BlockSpec((seq, LANES), lambda b, p: (b, 2 * w // LANES + p))],
        out_specs=pl.BlockSpec((seq, LANES), lambda b, p: (b, p)),
        out_shape=jax.ShapeDtypeStruct((t, w), BF16),
        scratch_shapes=[pltpu.VMEM((2, seq, LANES), BF16),
                        pltpu.VMEM((seq, LANES), BF16),
                        pltpu.VMEM((2, seq, LANES), BF16),
                        pltpu.VMEM((2 * SB_BLOCK, 2 * SB_BLOCK), BF16),
                        pltpu.VMEM((2, 4, SB_BLOCK, 2 * SB_BLOCK), F32),
                        pltpu.VMEM((2, 2, SB_BLOCK, 4 * SB_BLOCK), BF16),
                        pltpu.VMEM((4, SB_BLOCK, LANES), F32),
                        pltpu.VMEM((2, SB_BLOCK, LANES), F32)],
        compiler_params=_params("arbitrary", "arbitrary"),
        name="sb_mixer",
    )(proj, proj, proj)


def _pad_cols(w, width):
    return jnp.pad(w, ((0, 0), (0, width - w.shape[1])))


def kernel(x, mem, mem_norm, norm_pre_mix, norm_post_mix, norm_pre_ffn, norm_post_ffn, w_mem_kv, w_out,
           w_ffn_in, w_ffn_out, gdn_w_in, gdn_conv, gdn_a_log, gdn_dt_bias, gdn_out_norm, ml_w_in, ml_i_bias,
           ml_f_bias, ml_out_norm, sb_w_in):
    batch, seq, d = x.shape
    n_mem = mem.shape[1]
    depth = w_out.shape[0]
    d_ff = w_ffn_out.shape[1]
    h = x.reshape(batch * seq, d)
    mem2d = mem.reshape(batch * n_mem, d)
    tm = 512

    gdn_main = 3 * GDN_HEADS * GDN_DK + GDN_HEADS * LANES
    ml_main = 2 * ML_HEADS * ML_DQK + 2 * ML_HEADS * ML_DV
    sb_main = 3 * SB_HEADS * SB_DH

    for layer in range(depth):
        kind, j = layer % 3, layer // 3
        if kind == 0:
            w = gdn_w_in[j].astype(BF16)
            gates = w[:, gdn_main:gdn_main + 2 * GDN_HEADS]
            w = jnp.concatenate([w[:, :gdn_main], w[:, gdn_main + 2 * GDN_HEADS:], _pad_cols(gates, LANES)], axis=1)
            main = gdn_main
        elif kind == 1:
            w = ml_w_in[j].astype(BF16)
            gates = w[:, ml_main:ml_main + 2 * ML_HEADS]
            w = jnp.concatenate([w[:, :ml_main], w[:, ml_main + 2 * ML_HEADS:], _pad_cols(gates, LANES)], axis=1)
            main = ml_main
        else:
            w = sb_w_in[j].astype(BF16)
            main = sb_main
        if kind == 0:
            proj, gates_t = _norm_matmul(h, norm_pre_mix[layer], w, tm=tm, conv_w=gdn_conv[j],
                                         gate_col=main + MEM_W, seq=seq)
            ymix = _gdn_mixer(proj, gates_t, gdn_a_log[j], gdn_dt_bias[j], gdn_out_norm[j],
                              batch=batch, seq=seq, ab_col=main + MEM_W)
        elif kind == 1:
            proj, gates_t = _norm_matmul(h, norm_pre_mix[layer], w, tm=tm, gate_col=main + MEM_W)
            ymix = _mlstm_mixer(proj, gates_t, ml_i_bias[j], ml_f_bias[j], ml_out_norm[j],
                                batch=batch, seq=seq, if_col=main + MEM_W)
        else:
            proj = _norm_matmul(h, norm_pre_mix[layer], w, tm=tm)
            ymix = _sb_mixer(proj, batch=batch, seq=seq)
        kv = _norm_matmul(mem2d, mem_norm, w_mem_kv[layer].astype(BF16), tm=n_mem)
        h = _layer_tail(h, ymix, proj, kv, w_out[layer].astype(BF16), norm_post_mix[layer], norm_pre_ffn[layer],
                        w_ffn_in[layer].astype(BF16), w_ffn_out[layer].astype(BF16), norm_post_ffn[layer],
                        tm=tm, ff_chunk=256, seq=seq, n_mem=n_mem, qcol=main)
    return h.reshape(batch, seq, d)
```
